```python
import jax, jax.numpy as jnp
from jax import lax
import numpy as np

D_MODEL = 1024
BATCH = 16
SEQ = 2048
DEPTH = 4

GRID_W = 64
CTX_LEN = 256
HEAD_DIM = 64
ROPE_BASE = 10000.0
Q_BLOCK = 128
NORM_EPS = 1e-6
LN_EPS = 1e-5
CONV_CH = D_MODEL // 2
CONV_WIDTH = 31
GQA_HEADS = (D_MODEL // 2) // HEAD_DIM
GQA_KV_HEADS = GQA_HEADS // 4
GQA_GROUP = GQA_HEADS // GQA_KV_HEADS
GQA_Q_W = GQA_HEADS * HEAD_DIM
GQA_KV_W = GQA_KV_HEADS * HEAD_DIM
EVEN_IN = 2 * CONV_CH + GQA_Q_W + 2 * GQA_KV_W
EVEN_MIX = CONV_CH + GQA_Q_W
DIFF_HEADS = D_MODEL // (2 * HEAD_DIM)
DIFF_V_DIM = 2 * HEAD_DIM
DIFF_QK_W = DIFF_HEADS * 2 * HEAD_DIM
ODD_IN = 3 * DIFF_QK_W
FFN_DIM = (7 * D_MODEL) // 2
N_EXPERTS = 8
TOP_K = 2
N_EVEN = (DEPTH + 1) // 2
N_ODD = DEPTH // 2

kernel_name = 'hybrid_conv_gqa_diffattn_moe_dit'


def _rms_norm(x, g):
    xf = x.astype(jnp.float32)
    y = xf * lax.rsqrt(jnp.mean(xf * xf, axis=-1, keepdims=True) + NORM_EPS)
    return (y * g.astype(jnp.float32)).astype(x.dtype)


def _layer_norm(x, g, b):
    xf = x.astype(jnp.float32)
    mu = jnp.mean(xf, axis=-1, keepdims=True)
    var = jnp.mean(jnp.square(xf - mu), axis=-1, keepdims=True)
    y = (xf - mu) * lax.rsqrt(var + LN_EPS) * g.astype(jnp.float32) + b.astype(jnp.float32)
    return y.astype(x.dtype)


def _axial_rope_tables(n_lat):
    rows = n_lat // GRID_W
    r = jnp.broadcast_to(jnp.arange(rows, dtype=jnp.float32)[:, None], (rows, GRID_W)).reshape(-1)
    col = jnp.broadcast_to(jnp.arange(GRID_W, dtype=jnp.float32)[None, :], (rows, GRID_W)).reshape(-1)
    quarter = HEAD_DIM // 4
    inv_freq = ROPE_BASE ** (-jnp.arange(quarter, dtype=jnp.float32) / quarter)
    ang = jnp.stack([r[:, None] * inv_freq, col[:, None] * inv_freq], axis=1)
    return jnp.cos(ang), jnp.sin(ang)


def _apply_rope(x, cos, sin):
    shp = x.shape
    quarter = HEAD_DIM // 4
    xf = x.astype(jnp.float32).reshape(shp[:-1] + (2, 2, quarter))
    bshape = (1, shp[1]) + (1,) * (len(shp) - 3) + (2, quarter)
    cb = cos.reshape(bshape)
    sb = sin.reshape(bshape)
    x1 = xf[..., 0, :]
    x2 = xf[..., 1, :]
    out = jnp.stack([x1 * cb - x2 * sb, x2 * cb + x1 * sb], axis=-2)
    return out.reshape(shp).astype(x.dtype)


def _sweep_query_blocks(fn, q):
    bsz, n = q.shape[0], q.shape[1]
    nb = n // Q_BLOCK
    qb = jnp.moveaxis(q.reshape((bsz, nb, Q_BLOCK) + q.shape[2:]), 1, 0)
    out = jnp.moveaxis(lax.map(fn, qb), 0, 1)
    return out.reshape((bsz, n) + out.shape[3:])


def _gqa_block(qb, k, v):
    s = jnp.einsum('bqkgd,bskd->bkgqs', qb, k).astype(jnp.float32) * (HEAD_DIM ** -0.5)
    p = jax.nn.softmax(s, axis=-1).astype(v.dtype)
    return jnp.einsum('bkgqs,bskd->bqkgd', p, v)


def _diff_block(qb, k, v, lam):
    s = jnp.einsum('bqhcd,bshcd->bhcqs', qb, k).astype(jnp.float32) * (HEAD_DIM ** -0.5)
    p = jax.nn.softmax(s, axis=-1)
    a = p[:, :, 0] - lam * p[:, :, 1]
    return jnp.einsum('bhqs,bshe->bqhe', a.astype(v.dtype), v)


def _conv_branch(a, g, conv_w, ln_g, ln_b):
    y = a * jax.nn.sigmoid(g)
    pad = CONV_WIDTH // 2
    y = lax.conv_general_dilated(y, conv_w[:, None, :].astype(y.dtype), (1,), [(pad, pad)],
                                 dimension_numbers=('NWC', 'WIO', 'NWC'), feature_group_count=CONV_CH)
    return jax.nn.silu(_layer_norm(y, ln_g, ln_b))


def _even_mixer(h_lat, h_ctx, w_in, conv_w, ln_g, ln_b, qn_g, kn_g, w_out, cos, sin, need_ctx):
    bsz, n_lat, _ = h_lat.shape
    n_ctx = h_ctx.shape[1]
    cuts = [CONV_CH, 2 * CONV_CH, 2 * CONV_CH + GQA_Q_W, 2 * CONV_CH + GQA_Q_W + GQA_KV_W]
    a_l, g_l, q_l, k_l, v_l = jnp.split(h_lat @ w_in, cuts, axis=-1)
    a_c, g_c, q_c, k_c, v_c = jnp.split(h_ctx @ w_in, cuts, axis=-1)
    q_l = _apply_rope(_rms_norm(q_l.reshape(bsz, n_lat, GQA_HEADS, HEAD_DIM), qn_g), cos, sin)
    k_l = _apply_rope(_rms_norm(k_l.reshape(bsz, n_lat, GQA_KV_HEADS, HEAD_DIM), kn_g), cos, sin)
    v_l = v_l.reshape(bsz, n_lat, GQA_KV_HEADS, HEAD_DIM)
    k_c = _rms_norm(k_c.reshape(bsz, n_ctx, GQA_KV_HEADS, HEAD_DIM), kn_g)
    v_c = v_c.reshape(bsz, n_ctx, GQA_KV_HEADS, HEAD_DIM)
    keys = jnp.concatenate([k_l, k_c], axis=1)
    vals = jnp.concatenate([v_l, v_c], axis=1)
    q_l = q_l.reshape(bsz, n_lat, GQA_KV_HEADS, GQA_GROUP, HEAD_DIM)
    att_l = _sweep_query_blocks(lambda qb: _gqa_block(qb, keys, vals), q_l).reshape(bsz, n_lat, GQA_Q_W)
    conv_l = _conv_branch(a_l, g_l, conv_w, ln_g, ln_b)
    y_l = jnp.concatenate([conv_l, att_l], axis=-1) @ w_out
    y_c = None
    if need_ctx:
        q_c = _rms_norm(q_c.reshape(bsz, n_ctx, GQA_HEADS, HEAD_DIM), qn_g)
        q_c = q_c.reshape(bsz, n_ctx, GQA_KV_HEADS, GQA_GROUP, HEAD_DIM)
        att_c = _gqa_block(q_c, k_c, v_c).reshape(bsz, n_ctx, GQA_Q_W)
        conv_c = _conv_branch(a_c, g_c, conv_w, ln_g, ln_b)
        y_c = jnp.concatenate([conv_c, att_c], axis=-1) @ w_out
    return y_l, y_c


def _odd_mixer(h_lat, h_ctx, w_in, lam_p, subln_g, w_out, cos, sin, lam_init, need_ctx):
    def proj(h):
        b, n, _ = h.shape
        q, k, v = jnp.split(h @ w_in, [DIFF_QK_W, 2 * DIFF_QK_W], axis=-1)
        return (q.reshape(b, n, DIFF_HEADS, 2, HEAD_DIM), k.reshape(b, n, DIFF_HEADS, 2, HEAD_DIM),
                v.reshape(b, n, DIFF_HEADS, DIFF_V_DIM))

    q_l, k_l, v_l = proj(h_lat)
    q_c, k_c, v_c = proj(h_ctx)
    q_l = _apply_rope(q_l, cos, sin)
    k_l = _apply_rope(k_l, cos, sin)
    lp = lam_p.astype(jnp.float32)
    lam = jnp.exp(jnp.sum(lp[0] * lp[1])) - jnp.exp(jnp.sum(lp[2] * lp[3])) + lam_init
    keys = jnp.concatenate([k_l, k_c], axis=1)
    vals = jnp.concatenate([v_l, v_c], axis=1)

    def finish(o):
        o = _rms_norm(o, subln_g) * (1.0 - lam_init)
        return o.reshape(o.shape[0], o.shape[1], DIFF_HEADS * DIFF_V_DIM) @ w_out

    y_l = finish(_sweep_query_blocks(lambda qb: _diff_block(qb, keys, vals, lam), q_l))
    y_c = finish(_diff_block(q_c, k_c, v_c, lam)) if need_ctx else None
    return y_l, y_c


def _swiglu(h, wg, wu, wd):
    return (jax.nn.silu(h @ wg) * (h @ wu)) @ wd


def _moe(h, router_w, wg, wu, wd):
    logits = (h @ router_w).astype(jnp.float32)
    top_v, top_i = lax.top_k(logits, TOP_K)
    top_w = jax.nn.softmax(top_v, axis=-1)
    combine = jnp.sum(jax.nn.one_hot(top_i, N_EXPERTS, dtype=jnp.float32) * top_w[..., None], axis=-2)
    y = jnp.zeros_like(h)
    for e in range(N_EXPERTS):
        y = y + combine[..., e:e + 1].astype(h.dtype) * _swiglu(h, wg[e], wu[e], wd[e])
    return y


def setup_inputs(seed: int = 0) -> dict:
    key = jax.random.key(seed)
    ks = iter(jax.random.split(key, 32))
    f32 = jnp.float32
    D, F = D_MODEL, FFN_DIM

    def nrm(shape, scale):
        return jax.random.normal(next(ks), shape, f32) * scale

    def gain(shape):
        return 1.0 + nrm(shape, 0.02)

    return {
        'x': nrm((BATCH, SEQ, D), 1.0),
        'c': nrm((BATCH, D), 1.0),
        'ctx': nrm((BATCH, CTX_LEN, D), 1.0),
        'c_ctx': nrm((D,), 1.0),
        'ada_w': nrm((DEPTH, D, 6 * D), 0.5 * D ** -0.5),
        'ada_b': nrm((DEPTH, 6 * D), 0.02),
        'norm1_g': gain((DEPTH, D)),
        'norm2_g': gain((DEPTH, D)),
        'ev_w_in': nrm((N_EVEN, D, EVEN_IN), D ** -0.5),
        'ev_conv_w': nrm((N_EVEN, CONV_WIDTH, CONV_CH), CONV_WIDTH ** -0.5),
        'ev_ln_g': gain((N_EVEN, CONV_CH)),
        'ev_ln_b': nrm((N_EVEN, CONV_CH), 0.02),
        'ev_q_norm_g': gain((N_EVEN, HEAD_DIM)),
        'ev_k_norm_g': gain((N_EVEN, HEAD_DIM)),
        'ev_w_out': nrm((N_EVEN, EVEN_MIX, D), EVEN_MIX ** -0.5),
        'ev_ffn_wg': nrm((N_EVEN, D, F), D ** -0.5),
        'ev_ffn_wu': nrm((N_EVEN, D, F), D ** -0.5),
        'ev_ffn_wd': nrm((N_EVEN, F, D), F ** -0.5),
        'od_w_in': nrm((N_ODD, D, ODD_IN), D ** -0.5),
        'od_lam': nrm((N_ODD, 4, HEAD_DIM), 0.1),
        'od_subln_g': gain((N_ODD, DIFF_V_DIM)),
        'od_w_out': nrm((N_ODD, DIFF_HEADS * DIFF_V_DIM, D), (DIFF_HEADS * DIFF_V_DIM) ** -0.5),
        'od_router_w': nrm((N_ODD, D, N_EXPERTS), D ** -0.5),
        'od_moe_wg': nrm((N_ODD, N_EXPERTS, D, F), D ** -0.5),
        'od_moe_wu': nrm((N_ODD, N_EXPERTS, D, F), D ** -0.5),
        'od_moe_wd': nrm((N_ODD, N_EXPERTS, F, D), F ** -0.5),
        'final_norm_g': gain((D,)),
    }


def reference(x, c, ctx, c_ctx, ada_w, ada_b, norm1_g, norm2_g, ev_w_in, ev_conv_w, ev_ln_g, ev_ln_b,
              ev_q_norm_g, ev_k_norm_g, ev_w_out, ev_ffn_wg, ev_ffn_wu, ev_ffn_wd, od_w_in, od_lam,
              od_subln_g, od_w_out, od_router_w, od_moe_wg, od_moe_wu, od_moe_wd, final_norm_g):
    n_lat = x.shape[1]
    cos, sin = _axial_rope_tables(n_lat)
    s_lat = jax.nn.silu(c)
    s_ctx = jax.nn.silu(c_ctx)
    xl, xc = x, ctx
    for l in range(DEPTH):
        need_ctx = l < DEPTH - 1
        i = l // 2
        m_l = jnp.split((s_lat @ ada_w[l] + ada_b[l])[:, None, :], 6, axis=-1)
        m_c = jnp.split((s_ctx @ ada_w[l] + ada_b[l])[None, None, :], 6, axis=-1)
        h_l = _rms_norm(xl, norm1_g[l]) * (1.0 + m_l[1]) + m_l[0]
        h_c = _rms_norm(xc, norm1_g[l]) * (1.0 + m_c[1]) + m_c[0]
        if l % 2 == 0:
            y_l, y_c = _even_mixer(h_l, h_c, ev_w_in[i], ev_conv_w[i], ev_ln_g[i], ev_ln_b[i],
                                   ev_q_norm_g[i], ev_k_norm_g[i], ev_w_out[i], cos, sin, need_ctx)
        else:
            lam_init = 0.8 - 0.6 * float(np.exp(-0.3 * l))
            y_l, y_c = _odd_mixer(h_l, h_c, od_w_in[i], od_lam[i], od_subln_g[i], od_w_out[i],
                                  cos, sin, lam_init, need_ctx)
        xl = xl + m_l[2] * y_l
        if need_ctx:
            xc = xc + m_c[2] * y_c
        h_l = _rms_norm(xl, norm2_g[l]) * (1.0 + m_l[4]) + m_l[3]
        if l % 2 == 0:
            xl = xl + m_l[5] * _swiglu(h_l, ev_ffn_wg[i], ev_ffn_wu[i], ev_ffn_wd[i])
        else:
            xl = xl + m_l[5] * _moe(h_l, od_router_w[i], od_moe_wg[i], od_moe_wu[i], od_moe_wd[i])
        if need_ctx:
            h_c = _rms_norm(xc, norm2_g[l]) * (1.0 + m_c[4]) + m_c[3]
            if l % 2 == 0:
                xc = xc + m_c[5] * _swiglu(h_c, ev_ffn_wg[i], ev_ffn_wu[i], ev_ffn_wd[i])
            else:
                xc = xc + m_c[5] * _moe(h_c, od_router_w[i], od_moe_wg[i], od_moe_wu[i], od_moe_wd[i])
    return _rms_norm(xl, final_norm_g)
```

```python
import functools
import math

import jax
import jax.numpy as jnp
from jax import lax
from jax.experimental import pallas as pl
from jax.experimental.pallas import tpu as pltpu

F32 = jnp.float32
BF16 = jnp.bfloat16

HEAD_DIM = 64
GRID_W = 64
ROPE_BASE = 10000.0
NORM_EPS = 1e-6
LN_EPS = 1e-5
CONV_WIDTH = 31
N_EXPERTS = 8
LANES = 128
MOD_ROWS = 24
VMEM_LIMIT = 56 * 1024 * 1024
ATT_SCALE = HEAD_DIM ** -0.5
TQ = 256
CONV_CHUNK = 64


def _params(sem):
    return pltpu.CompilerParams(dimension_semantics=sem, vmem_limit_bytes=VMEM_LIMIT)


def _pick_tile(total, target):
    best = 8
    for cand in range(8, min(total, target) + 1, 8):
        if total % cand == 0:
            best = cand
    return best


def _adaln_kernel(s_ref, w_ref, b_ref, o_ref):
    s = s_ref[...]
    s = s * jax.nn.sigmoid(s)
    o_ref[0] = jnp.dot(s, w_ref[0], precision=lax.Precision.HIGHEST,
                       preferred_element_type=F32) + b_ref[0]


def _adaln(cond, ada_w, ada_b):
    depth, d, six_d = ada_w.shape
    tn = six_d // 4
    return pl.pallas_call(
        _adaln_kernel,
        grid=(depth, six_d // tn),
        in_specs=[pl.BlockSpec((MOD_ROWS, d), lambda l, n: (0, 0)),
                  pl.BlockSpec((1, d, tn), lambda l, n: (l, 0, n)),
                  pl.BlockSpec((1, 1, tn), lambda l, n: (l, 0, n))],
        out_specs=pl.BlockSpec((1, MOD_ROWS, tn), lambda l, n: (l, 0, n)),
        out_shape=jax.ShapeDtypeStruct((depth, MOD_ROWS, six_d), F32),
        compiler_params=_params(("arbitrary", "arbitrary")),
        name="adaln",
    )(cond, ada_w, ada_b.reshape(depth, 1, six_d))


def _is_ctx_rows(t, tt, n_lat):
    rows = t * tt + lax.broadcasted_iota(jnp.int32, (tt, 1), 0)
    return rows >= n_lat


def _mod_pick(is_ctx, modl_ref, modc_ref, idx, d):
    return jnp.where(is_ctx, modc_ref[:, idx * d:(idx + 1) * d], modl_ref[:, idx * d:(idx + 1) * d])


def _norm_mod(x, g, is_ctx, modl_ref, modc_ref, shift_idx):
    d = x.shape[-1]
    r = lax.rsqrt(jnp.mean(x * x, axis=-1, keepdims=True) + NORM_EPS)
    shift = _mod_pick(is_ctx, modl_ref, modc_ref, shift_idx, d)
    scale = _mod_pick(is_ctx, modl_ref, modc_ref, shift_idx + 1, d)
    return (x * r * g) * (1.0 + scale) + shift


def _rope(xs, cos, sin_a, sin_b):
    return xs * cos + pltpu.roll(xs, LANES - 16, 1) * sin_a + pltpu.roll(xs, 16, 1) * sin_b


def _rope_tables(n_lat, n_ctx):
    rows = n_lat // GRID_W
    r = jnp.broadcast_to(jnp.arange(rows, dtype=F32)[:, None], (rows, GRID_W)).reshape(-1)
    col = jnp.broadcast_to(jnp.arange(GRID_W, dtype=F32)[None, :], (rows, GRID_W)).reshape(-1)
    quarter = HEAD_DIM // 4
    inv_freq = ROPE_BASE ** (-jnp.arange(quarter, dtype=F32) / quarter)
    ang_r = r[:, None] * inv_freq
    ang_c = col[:, None] * inv_freq
    zeros = jnp.zeros_like(ang_r)
    cos64 = jnp.concatenate([jnp.cos(ang_r), jnp.cos(ang_r), jnp.cos(ang_c), jnp.cos(ang_c)], axis=1)
    sin_a64 = jnp.concatenate([-jnp.sin(ang_r), zeros, -jnp.sin(ang_c), zeros], axis=1)
    sin_b64 = jnp.concatenate([zeros, jnp.sin(ang_r), zeros, jnp.sin(ang_c)], axis=1)

    def full(tab, ctx_val):
        tab = jnp.concatenate([tab, jnp.full((n_ctx, HEAD_DIM), ctx_val, F32)], axis=0)
        return jnp.tile(tab, (1, LANES // HEAD_DIM))

    return full(cos64, 1.0), full(sin_a64, 0.0), full(sin_b64, 0.0)


def _head_sum_matrix():
    i = jnp.arange(LANES) // HEAD_DIM
    return (i[:, None] == i[None, :]).astype(BF16)


def _mod_specs(layer, n_batch, six_d, nargs):
    if nargs == 2:
        lat = lambda b, t: (layer * MOD_ROWS + b, 0, 0)
        ctx = lambda b, t: (layer * MOD_ROWS + n_batch, 0, 0)
    elif nargs == 3:
        lat = lambda b, t, f: (layer * MOD_ROWS + b, 0, 0)
        ctx = lambda b, t, f: (layer * MOD_ROWS + n_batch, 0, 0)
    else:
        lat = lambda b, t, e, f: (layer * MOD_ROWS + b, 0, 0)
        ctx = lambda b, t, e, f: (layer * MOD_ROWS + n_batch, 0, 0)
    return [pl.BlockSpec((None, 1, six_d), lat), pl.BlockSpec((None, 1, six_d), ctx)]


def _even_in_kernel(x_ref, modl_ref, modc_ref, g_ref, w_ref, qg_ref, kg_ref, cos_ref, sa_ref, sb_ref,
                    hs_ref, ag_ref, q_ref, k_ref, v_ref, *, tt, n_lat):
    d = x_ref.shape[-1]
    conv2 = d
    qw = d // 2
    kvw = (w_ref.shape[1] - conv2 - qw) // 2
    is_ctx = _is_ctx_rows(pl.program_id(1), tt, n_lat)
    h = _norm_mod(x_ref[0], g_ref[...], is_ctx, modl_ref, modc_ref, 0).astype(BF16)
    half = conv2 // 2
    for c in range(2):
        ag_ref[0, :, c * half:(c + 1) * half] = jnp.dot(
            h, w_ref[:, c * half:(c + 1) * half], preferred_element_type=F32)
    cos, sa, sb = cos_ref[...], sa_ref[...], sb_ref[...]
    hs = hs_ref[...]

    def head_norm_rope(ys, gain):
        ss = jnp.dot((ys * ys).astype(BF16), hs, preferred_element_type=F32)
        yn = ys * lax.rsqrt(ss * (1.0 / HEAD_DIM) + NORM_EPS) * gain
        return _rope(yn, cos, sa, sb).astype(BF16)

    q = jnp.dot(h, w_ref[:, conv2:conv2 + qw], preferred_element_type=F32)
    for c in range(qw // LANES):
        q_ref[0, :, c * LANES:(c + 1) * LANES] = head_norm_rope(q[:, c * LANES:(c + 1) * LANES], qg_ref[...])
    kv = jnp.dot(h, w_ref[:, conv2 + qw:], preferred_element_type=F32)
    for c in range(kvw // LANES):
        k_ref[0, :, c * LANES:(c + 1) * LANES] = head_norm_rope(kv[:, c * LANES:(c + 1) * LANES], kg_ref[...])
    v_ref[0] = kv[:, kvw:].astype(BF16)


def _even_in(xs, mod3, layer, g1, w_in, qg, kg, tabs, hs, n_lat):
    n_batch, t_all, d = xs.shape
    tt = _pick_tile(t_all, 768)
    qw = d // 2
    kvw = (w_in.shape[1] - d - qw) // 2
    full2 = lambda b, t: (0, 0)
    tok = lambda b, t: (b, t, 0)
    tab_spec = pl.BlockSpec((tt, LANES), lambda b, t: (t, 0))
    return pl.pallas_call(
        functools.partial(_even_in_kernel, tt=tt, n_lat=n_lat),
        grid=(n_batch, t_all // tt),
        in_specs=[pl.BlockSpec((1, tt, d), tok)] + _mod_specs(layer, n_batch, 6 * d, 2) + [
            pl.BlockSpec((1, d), full2),
            pl.BlockSpec(w_in.shape, full2),
            pl.BlockSpec((1, LANES), full2), pl.BlockSpec((1, LANES), full2),
            tab_spec, tab_spec, tab_spec,
            pl.BlockSpec((LANES, LANES), full2)],
        out_specs=[pl.BlockSpec((1, tt, d), tok), pl.BlockSpec((1, tt, qw), tok),
                   pl.BlockSpec((1, tt, kvw), tok), pl.BlockSpec((1, tt, kvw), tok)],
        out_shape=[jax.ShapeDtypeStruct((n_batch, t_all, d), F32),
                   jax.ShapeDtypeStruct((n_batch, t_all, qw), BF16),
                   jax.ShapeDtypeStruct((n_batch, t_all, kvw), BF16),
                   jax.ShapeDtypeStruct((n_batch, t_all, kvw), BF16)],
        compiler_params=_params(("arbitrary", "arbitrary")),
        name="even_in",
    )(xs, mod3, mod3, g1, w_in, qg, kg, *tabs, hs)


def _gqa_kernel(q_ref, k_ref, v_ref, o_ref, k4_ref, v4_ref, *, n_lat, n_ctx):
    kv = pl.program_id(1)
    qi = pl.program_id(2)
    t_all = n_lat + n_ctx
    group = q_ref.shape[-1] // HEAD_DIM

    @pl.when(qi == 0)
    def _():
        lane = lax.broadcasted_iota(jnp.int32, (t_all, LANES), 1)
        keep = jnp.logical_xor(lane < HEAD_DIM, kv == 1)
        for src, dst in ((k_ref, k4_ref), (v_ref, v4_ref)):
            a = src[0].astype(F32)
            rep = jnp.where(keep, a, pltpu.roll(a, HEAD_DIM, 1)).astype(BF16)
            for c in range(dst.shape[-1] // LANES):
                dst[:, c * LANES:(c + 1) * LANES] = rep

    def attend(k0, nk):
        q = q_ref[0]
        head = lax.broadcasted_iota(jnp.int32, q.shape, 1) // HEAD_DIM
        qs = jnp.concatenate([jnp.where(head == g, q, jnp.zeros_like(q)) for g in range(group)], axis=0)
        s = lax.dot_general(qs, k4_ref[k0:k0 + nk, :], (((1,), (1,)), ((), ())),
                            preferred_element_type=F32)
        m = jnp.max(s, axis=-1, keepdims=True)
        p = jnp.exp(s - m)
        l = jnp.sum(p, axis=-1, keepdims=True)
        o = jnp.dot(p.astype(BF16), v4_ref[k0:k0 + nk, :], preferred_element_type=F32) / l
        out = jnp.zeros(q.shape, F32)
        for g in range(group):
            out = out + jnp.where(head == g, o[g * TQ:(g + 1) * TQ], 0.0)
        o_ref[0] = out.astype(BF16)

    @pl.when(qi < n_lat // TQ)
    def _():
        attend(0, t_all)

    @pl.when(qi >= n_lat // TQ)
    def _():
        attend(n_lat, n_ctx)


def _gqa(q, k, v, n_lat, n_ctx):
    n_batch, t_all, qw = q.shape
    kvw = k.shape[-1]
    n_kv = kvw // HEAD_DIM
    gw = qw // n_kv
    return pl.pallas_call(
        functools.partial(_gqa_kernel, n_lat=n_lat, n_ctx=n_ctx),
        grid=(n_batch, n_kv, t_all // TQ),
        in_specs=[pl.BlockSpec((1, TQ, gw), lambda b, h, i: (b, i, h)),
                  pl.BlockSpec((1, t_all, kvw), lambda b, h, i: (b, 0, 0)),
                  pl.BlockSpec((1, t_all, kvw), lambda b, h, i: (b, 0, 0))],
        out_specs=pl.BlockSpec((1, TQ, gw), lambda b, h, i: (b, i, h)),
        out_shape=jax.ShapeDtypeStruct((n_batch, t_all, qw), BF16),
        scratch_shapes=[pltpu.VMEM((t_all, gw), BF16), pltpu.VMEM((t_all, gw), BF16)],
        compiler_params=_params(("arbitrary", "arbitrary", "arbitrary")),
        name="gqa",
    )(q, k, v)


def _conv_kernel(ag_ref, w_ref, lng_ref, lnb_ref, o_ref, ypad_ref, *, n_lat, n_ctx):
    ch = o_ref.shape[-1]
    pad = CONV_WIDTH // 2
    halo = 16
    lat0 = halo
    ctx0 = 2 * halo + n_lat
    zeros = jnp.zeros((halo, ch), F32)
    ypad_ref[0:halo, :] = zeros
    ypad_ref[halo + n_lat:ctx0, :] = zeros
    ypad_ref[ctx0 + n_ctx:ctx0 + n_ctx + halo, :] = zeros

    def glu(src0, dst0, n):
        def body(i, carry):
            r = pl.multiple_of(i * CONV_CHUNK, CONV_CHUNK)
            a = ag_ref[0, pl.ds(src0 + r, CONV_CHUNK), 0:ch]
            g = ag_ref[0, pl.ds(src0 + r, CONV_CHUNK), ch:2 * ch]
            ypad_ref[pl.ds(dst0 + r, CONV_CHUNK), :] = a * jax.nn.sigmoid(g)
            return carry
        lax.fori_loop(0, n // CONV_CHUNK, body, 0)

    glu(0, lat0, n_lat)
    glu(n_lat, ctx0, n_ctx)

    def conv(src0, dst0, n):
        def body(i, carry):
            r = pl.multiple_of(i * CONV_CHUNK, CONV_CHUNK)
            accs = []
            for c in range(ch // LANES):
                cols = slice(c * LANES, (c + 1) * LANES)
                win = ypad_ref[pl.ds(src0 - halo + r, CONV_CHUNK + 2 * halo), cols]
                acc = jnp.zeros((CONV_CHUNK, LANES), F32)
                for s in range(8):
                    taps = [j for j in range(CONV_WIDTH) if (j + halo - pad) % 8 == s]
                    if not taps:
                        continue
                    sh = win[s:s + CONV_CHUNK + 2 * halo - 8]
                    for j in taps:
                        a8 = (j + halo - pad) - s
                        acc = acc + w_ref[j:j + 1, cols] * sh[a8:a8 + CONV_CHUNK]
                accs.append(acc)
            mu = sum(jnp.sum(a, axis=-1, keepdims=True) for a in accs) * (1.0 / ch)
            cens = [a - mu for a in accs]
            var = sum(jnp.sum(cn * cn, axis=-1, keepdims=True) for cn in cens) * (1.0 / ch)
            rstd = lax.rsqrt(var + LN_EPS)
            for c, cn in enumerate(cens):
                cols = slice(c * LANES, (c + 1) * LANES)
                yn = cn * rstd * lng_ref[:, cols] + lnb_ref[:, cols]
                o_ref[0, pl.ds(dst0 + r, CONV_CHUNK), cols] = (yn * jax.nn.sigmoid(yn)).astype(BF16)
            return carry
        lax.fori_loop(0, n // CONV_CHUNK, body, 0)

    conv(lat0, 0, n_lat)
    conv(ctx0, n_lat, n_ctx)


def _conv(ag, conv_w, ln_g, ln_b, n_lat, n_ctx):
    n_batch, t_all, two_ch = ag.shape
    ch = two_ch // 2
    full2 = lambda b: (0, 0)
    return pl.pallas_call(
        functools.partial(_conv_kernel, n_lat=n_lat, n_ctx=n_ctx),
        grid=(n_batch,),
        in_specs=[pl.BlockSpec((1, t_all, two_ch), lambda b: (b, 0, 0)),
                  pl.BlockSpec(conv_w.shape, full2),
                  pl.BlockSpec((1, ch), full2), pl.BlockSpec((1, ch), full2)],
        out_specs=pl.BlockSpec((1, t_all, ch), lambda b: (b, 0, 0)),
        out_shape=jax.ShapeDtypeStruct((n_batch, t_all, ch), BF16),
        scratch_shapes=[pltpu.VMEM((t_all + 48, ch), F32)],
        compiler_params=_params(("arbitrary",)),
        name="conv",
    )(ag, conv_w, ln_g, ln_b)


def _odd_in_kernel(x_ref, modl_ref, modc_ref, g_ref, w_ref, cos_ref, sa_ref, sb_ref,
                   q_ref, k_ref, v_ref, *, tt, n_lat):
    d = x_ref.shape[-1]
    is_ctx = _is_ctx_rows(pl.program_id(1), tt, n_lat)
    h = _norm_mod(x_ref[0], g_ref[...], is_ctx, modl_ref, modc_ref, 0).astype(BF16)
    cos, sa, sb = cos_ref[...], sa_ref[...], sb_ref[...]
    cw = 512
    for part, (dst, scale) in enumerate(((q_ref, ATT_SCALE), (k_ref, None), (v_ref, None))):
        for c0 in range(0, d, cw):
            y = jnp.dot(h, w_ref[:, part * d + c0:part * d + c0 + cw], preferred_element_type=F32)
            if part == 2:
                dst[0, :, c0:c0 + cw] = y.astype(BF16)
                continue
            for c in range(cw // LANES):
                ys = _rope(y[:, c * LANES:(c + 1) * LANES], cos, sa, sb)
                if scale is not None:
                    ys = ys * scale
                dst[0, :, c0 + c * LANES:c0 + (c + 1) * LANES] = ys.astype(BF16)


def _odd_in(xs, mod3, layer, g1, w_in, tabs, n_lat):
    n_batch, t_all, d = xs.shape
    tt = _pick_tile(t_all, 768)
    full2 = lambda b, t: (0, 0)
    tok = lambda b, t: (b, t, 0)
    tab_spec = pl.BlockSpec((tt, LANES), lambda b, t: (t, 0))
    return pl.pallas_call(
        functools.partial(_odd_in_kernel, tt=tt, n_lat=n_lat),
        grid=(n_batch, t_all // tt),
        in_specs=[pl.BlockSpec((1, tt, d), tok)] + _mod_specs(layer, n_batch, 6 * d, 2) + [
            pl.BlockSpec((1, d), full2),
            pl.BlockSpec(w_in.shape, full2),
            tab_spec, tab_spec, tab_spec],
        out_specs=[pl.BlockSpec((1, tt, d), tok)] * 3,
        out_shape=[jax.ShapeDtypeStruct((n_batch, t_all, d), BF16)] * 3,
        compiler_params=_params(("arbitrary", "arbitrary")),
        name="odd_in",
    )(xs, mod3, mod3, g1, w_in, *tabs)


def _diff_kernel(q_ref, k_ref, v_ref, lam_ref, sg_ref, o_ref, *, n_lat, n_ctx, lam_init):
    qi = pl.program_id(2)
    t_all = n_lat + n_ctx
    lp = lam_ref[...]
    lam = (jnp.exp(jnp.sum(lp[0:1] * lp[1:2], axis=-1, keepdims=True))
           - jnp.exp(jnp.sum(lp[2:3] * lp[3:4], axis=-1, keepdims=True)) + lam_init)

    def attend(k0, nk):
        q = q_ref[0]
        first = lax.broadcasted_iota(jnp.int32, q.shape, 1) < HEAD_DIM
        zero = jnp.zeros_like(q)
        qs = jnp.concatenate([jnp.where(first, q, zero), jnp.where(first, zero, q)], axis=0)
        s = lax.dot_general(qs, k_ref[0, k0:k0 + nk, :], (((1,), (1,)), ((), ())),
                            preferred_element_type=F32)
        m = jnp.max(s, axis=-1, keepdims=True)
        p = jnp.exp(s - m)
        l = jnp.sum(p, axis=-1, keepdims=True)
        p = p / l
        a = p[0:TQ] - lam * p[TQ:2 * TQ]
        o = jnp.dot(a.astype(BF16), v_ref[0, k0:k0 + nk, :], preferred_element_type=F32)
        r = lax.rsqrt(jnp.mean(o * o, axis=-1, keepdims=True) + NORM_EPS)
        o_ref[0] = ((o * r * sg_ref[...]) * (1.0 - lam_init)).astype(BF16)

    @pl.when(qi < n_lat // TQ)
    def _():
        attend(0, t_all)

    @pl.when(qi >= n_lat // TQ)
    def _():
        attend(n_lat, n_ctx)


def _diff_attn(q, k, v, lam_p, subln_g, lam_init, n_lat, n_ctx):
    n_batch, t_all, d = q.shape
    hw = 2 * HEAD_DIM
    full2 = lambda b, h, i: (0, 0)
    return pl.pallas_call(
        functools.partial(_diff_kernel, n_lat=n_lat, n_ctx=n_ctx, lam_init=lam_init),
        grid=(n_batch, d // hw, t_all // TQ),
        in_specs=[pl.BlockSpec((1, TQ, hw), lambda b, h, i: (b, i, h)),
                  pl.BlockSpec((1, t_all, hw), lambda b, h, i: (b, 0, h)),
                  pl.BlockSpec((1, t_all, hw), lambda b, h, i: (b, 0, h)),
                  pl.BlockSpec(lam_p.shape, full2),
                  pl.BlockSpec((1, hw), full2)],
        out_specs=pl.BlockSpec((1, TQ, hw), lambda b, h, i: (b, i, h)),
        out_shape=jax.ShapeDtypeStruct((n_batch, t_all, d), BF16),
        compiler_params=_params(("arbitrary", "arbitrary", "arbitrary")),
        name="diff_attn",
    )(q, k, v, lam_p, subln_g)


def _out_kernel(*refs, tt, n_lat, n_parts):
    x_ref, modl_ref, modc_ref = refs[0:3]
    parts = refs[3:3 + n_parts]
    w_ref = refs[3 + n_parts]
    o_ref = refs[4 + n_parts]
    d = x_ref.shape[-1]
    is_ctx = _is_ctx_rows(pl.program_id(1), tt, n_lat)
    gate = _mod_pick(is_ctx, modl_ref, modc_ref, 2, d)
    y = jnp.zeros((tt, d), F32)
    row = 0
    for p in parts:
        kp = p.shape[-1]
        y = y + jnp.dot(p[0], w_ref[row:row + kp, :], preferred_element_type=F32)
        row += kp
    o_ref[0] = x_ref[0] + gate * y


def _out_proj(xs, mod3, layer, parts, w_out, n_lat):
    n_batch, t_all, d = xs.shape
    tt = _pick_tile(t_all, 768)
    tok = lambda b, t: (b, t, 0)
    return pl.pallas_call(
        functools.partial(_out_kernel, tt=tt, n_lat=n_lat, n_parts=len(parts)),
        grid=(n_batch, t_all // tt),
        in_specs=[pl.BlockSpec((1, tt, d), tok)] + _mod_specs(layer, n_batch, 6 * d, 2)
        + [pl.BlockSpec((1, tt, p.shape[-1]), tok) for p in parts]
        + [pl.BlockSpec(w_out.shape, lambda b, t: (0, 0))],
        out_specs=pl.BlockSpec((1, tt, d), tok),
        out_shape=jax.ShapeDtypeStruct(xs.shape, F32),
        compiler_params=_params(("arbitrary", "arbitrary")),
        name="out_proj",
    )(xs, mod3, mod3, *parts, w_out)


def _ffn_kernel(x_ref, modl_ref, modc_ref, g_ref, wg_ref, wu_ref, wd_ref, o_ref, h_ref, acc_ref, *, tt, n_lat):
    f = pl.program_id(2)
    d = x_ref.shape[-1]
    is_ctx = _is_ctx_rows(pl.program_id(1), tt, n_lat)

    @pl.when(f == 0)
    def _():
        h_ref[...] = _norm_mod(x_ref[0], g_ref[...], is_ctx, modl_ref, modc_ref, 3).astype(BF16)
        acc_ref[...] = jnp.zeros_like(acc_ref)

    h = h_ref[...]
    g = jnp.dot(h, wg_ref[...], preferred_element_type=F32)
    u = jnp.dot(h, wu_ref[...], preferred_element_type=F32)
    act = (g * jax.nn.sigmoid(g) * u).astype(BF16)
    acc_ref[...] += jnp.dot(act, wd_ref[...], preferred_element_type=F32)

    @pl.when(f == pl.num_programs(2) - 1)
    def _():
        gate = _mod_pick(is_ctx, modl_ref, modc_ref, 5, d)
        o_ref[0] = x_ref[0] + gate * acc_ref[...]


def _ffn(xs, mod3, layer, g2, wg, wu, wd, n_lat):
    n_batch, t_all, d = xs.shape
    ffn = wg.shape[-1]
    tt = _pick_tile(t_all, 768)
    tf = 512
    tok = lambda b, t, f: (b, t, 0)
    return pl.pallas_call(
        functools.partial(_ffn_kernel, tt=tt, n_lat=n_lat),
        grid=(n_batch, t_all // tt, ffn // tf),
        in_specs=[pl.BlockSpec((1, tt, d), tok)] + _mod_specs(layer, n_batch, 6 * d, 3) + [
            pl.BlockSpec((1, d), lambda b, t, f: (0, 0)),
            pl.BlockSpec((d, tf), lambda b, t, f: (0, f)),
            pl.BlockSpec((d, tf), lambda b, t, f: (0, f)),
            pl.BlockSpec((tf, d), lambda b, t, f: (f, 0))],
        out_specs=pl.BlockSpec((1, tt, d), tok),
        out_shape=jax.ShapeDtypeStruct(xs.shape, F32),
        scratch_shapes=[pltpu.VMEM((tt, d), BF16), pltpu.VMEM((tt, d), F32)],
        compiler_params=_params(("arbitrary", "arbitrary", "arbitrary")),
        name="ffn",
    )(xs, mod3, mod3, g2, wg, wu, wd)


def _top2_combine(logits):
    lane = lax.broadcasted_iota(jnp.int32, logits.shape, 1).astype(F32)
    neg = jnp.float32(-jnp.inf)
    lg = jnp.where(lane < N_EXPERTS, logits, neg)
    m1 = jnp.max(lg, axis=-1, keepdims=True)
    i1 = jnp.min(jnp.where(lg == m1, lane, float(LANES)), axis=-1, keepdims=True)
    lg2 = jnp.where(lane == i1, neg, lg)
    m2 = jnp.max(lg2, axis=-1, keepdims=True)
    i2 = jnp.min(jnp.where(lg2 == m2, lane, float(LANES)), axis=-1, keepdims=True)
    e2 = jnp.exp(m2 - m1)
    den = 1.0 + e2
    return jnp.where(lane == i1, 1.0 / den, 0.0) + jnp.where(lane == i2, e2 / den, 0.0)


def _moe_kernel(x_ref, modl_ref, modc_ref, g_ref, rw_ref, wg_ref, wu_ref, wd_ref, o_ref,
                h_ref, cw_ref, acc_ref, *, tt, n_lat):
    e = pl.program_id(2)
    f = pl.program_id(3)
    d = x_ref.shape[-1]
    is_ctx = _is_ctx_rows(pl.program_id(1), tt, n_lat)

    @pl.when((e == 0) & (f == 0))
    def _():
        h = _norm_mod(x_ref[0], g_ref[...], is_ctx, modl_ref, modc_ref, 3)
        h_ref[...] = h.astype(BF16)
        logits = jnp.dot(h, rw_ref[...], precision=lax.Precision.HIGHEST, preferred_element_type=F32)
        cw_ref[...] = _top2_combine(logits)
        acc_ref[...] = jnp.zeros_like(acc_ref)

    lane = lax.broadcasted_iota(jnp.int32, cw_ref.shape, 1)
    cwe = jnp.sum(jnp.where(lane == e, cw_ref[...], 0.0), axis=-1, keepdims=True)
    h = h_ref[...]
    g = jnp.dot(h, wg_ref[0], preferred_element_type=F32)
    u = jnp.dot(h, wu_ref[0], preferred_element_type=F32)
    act = (g * jax.nn.sigmoid(g) * u * cwe).astype(BF16)
    acc_ref[...] += jnp.dot(act, wd_ref[0], preferred_element_type=F32)

    @pl.when((e == pl.num_programs(2) - 1) & (f == pl.num_programs(3) - 1))
    def _():
        gate = _mod_pick(is_ctx, modl_ref, modc_ref, 5, d)
        o_ref[0] = x_ref[0] + gate * acc_ref[...]


def _moe(xs, mod3, layer, g2, router_w, wg, wu, wd, n_lat):
    n_batch, t_all, d = xs.shape
    n_exp, _, ffn = wg.shape
    tt = _pick_tile(t_all, 768)
    tf = 512
    tok = lambda b, t, e, f: (b, t, 0)
    rw = jnp.zeros((d, LANES), F32).at[:, :n_exp].set(router_w)
    return pl.pallas_call(
        functools.partial(_moe_kernel, tt=tt, n_lat=n_lat),
        grid=(n_batch, t_all // tt, n_exp, ffn // tf),
        in_specs=[pl.BlockSpec((1, tt, d), tok)] + _mod_specs(layer, n_batch, 6 * d, 4) + [
            pl.BlockSpec((1, d), lambda b, t, e, f: (0, 0)),
            pl.BlockSpec((d, LANES), lambda b, t, e, f: (0, 0)),
            pl.BlockSpec((1, d, tf), lambda b, t, e, f: (e, 0, f)),
            pl.BlockSpec((1, d, tf), lambda b, t, e, f: (e, 0, f)),
            pl.BlockSpec((1, tf, d), lambda b, t, e, f: (e, f, 0))],
        out_specs=pl.BlockSpec((1, tt, d), tok),
        out_shape=jax.ShapeDtypeStruct(xs.shape, F32),
        scratch_shapes=[pltpu.VMEM((tt, d), BF16), pltpu.VMEM((tt, LANES), F32), pltpu.VMEM((tt, d), F32)],
        compiler_params=_params(("arbitrary", "arbitrary", "arbitrary", "arbitrary")),
        name="moe",
    )(xs, mod3, mod3, g2, rw, wg, wu, wd)


def _final_kernel(x_ref, g_ref, o_ref):
    x = x_ref[0]
    o_ref[0] = x * lax.rsqrt(jnp.mean(x * x, axis=-1, keepdims=True) + NORM_EPS) * g_ref[...]


def _final_norm(xs, g, n_lat):
    n_batch, _, d = xs.shape
    tt = _pick_tile(n_lat, 256)
    return pl.pallas_call(
        _final_kernel,
        grid=(n_batch, n_lat // tt),
        in_specs=[pl.BlockSpec((1, tt, d), lambda b, t: (b, t, 0)), pl.BlockSpec((1, d), lambda b, t: (0, 0))],
        out_specs=pl.BlockSpec((1, tt, d), lambda b, t: (b, t, 0)),
        out_shape=jax.ShapeDtypeStruct((n_batch, n_lat, d), F32),
        compiler_params=_params(("arbitrary", "arbitrary")),
        name="final_norm",
    )(xs, g)


def kernel(x, c, ctx, c_ctx, ada_w, ada_b, norm1_g, norm2_g, ev_w_in, ev_conv_w, ev_ln_g, ev_ln_b,
           ev_q_norm_g, ev_k_norm_g, ev_w_out, ev_ffn_wg, ev_ffn_wu, ev_ffn_wd, od_w_in, od_lam,
           od_subln_g, od_w_out, od_router_w, od_moe_wg, od_moe_wu, od_moe_wd, final_norm_g):
    n_batch, n_lat, d = x.shape
    n_ctx = ctx.shape[1]
    depth = ada_w.shape[0]
    assert n_batch + 1 <= MOD_ROWS and n_lat % TQ == 0 and n_ctx % TQ == 0 and d % LANES == 0

    cond = jnp.zeros((MOD_ROWS, d), F32).at[:n_batch].set(c).at[n_batch].set(c_ctx)
    mod3 = _adaln(cond, ada_w, ada_b).reshape(depth * MOD_ROWS, 1, 6 * d)
    tabs = _rope_tables(n_lat, n_ctx)
    hs = _head_sum_matrix()
    xs = jnp.concatenate([x, ctx], axis=1)

    for layer in range(depth):
        i = layer // 2
        g1 = norm1_g[layer].reshape(1, d)
        g2 = norm2_g[layer].reshape(1, d)
        if layer % 2 == 0:
            rep = LANES // HEAD_DIM
            qg = (jnp.tile(ev_q_norm_g[i], rep) * ATT_SCALE).reshape(1, LANES)
            kg = jnp.tile(ev_k_norm_g[i], rep).reshape(1, LANES)
            ag, q, k, v = _even_in(xs, mod3, layer, g1, ev_w_in[i].astype(BF16), qg, kg, tabs, hs, n_lat)
            att = _gqa(q, k, v, n_lat, n_ctx)
            cv = _conv(ag, ev_conv_w[i], ev_ln_g[i].reshape(1, -1), ev_ln_b[i].reshape(1, -1), n_lat, n_ctx)
            xs = _out_proj(xs, mod3, layer, [cv, att], ev_w_out[i].astype(BF16), n_lat)
            xs = _ffn(xs, mod3, layer, g2, ev_ffn_wg[i].astype(BF16), ev_ffn_wu[i].astype(BF16),
                      ev_ffn_wd[i].astype(BF16), n_lat)
        else:
            lam_init = 0.8 - 0.6 * math.exp(-0.3 * layer)
            q, k, v = _odd_in(xs, mod3, layer, g1, od_w_in[i].astype(BF16), tabs, n_lat)
            att = _diff_attn(q, k, v, od_lam[i], od_subln_g[i].reshape(1, -1), lam_init, n_lat, n_ctx)
            xs = _out_proj(xs, mod3, layer, [att], od_w_out[i].astype(BF16), n_lat)
            xs = _moe(xs, mod3, layer, g2, od_router_w[i], od_moe_wg[i].astype(BF16),
                      od_moe_wu[i].astype(BF16), od_moe_wd[i].astype(BF16), n_lat)
    return _final_norm(xs, final_norm_g.reshape(1, d), n_lat)
```

```python
import functools
import math

import jax
import jax.numpy as jnp
from jax import lax
from jax.experimental import pallas as pl
from jax.experimental.pallas import tpu as pltpu

F32 = jnp.float32
BF16 = jnp.bfloat16

HEAD_DIM = 64
GRID_W = 64
ROPE_BASE = 10000.0
NORM_EPS = 1e-6
LN_EPS = 1e-5
CONV_WIDTH = 31
N_EXPERTS = 8
LANES = 128
MOD_ROWS = 24
VMEM_LIMIT = 56 * 1024 * 1024
ATT_SCALE = HEAD_DIM ** -0.5 * math.log2(math.e)
TQ = 256
CONV_CHUNK = 64


def _params(sem):
    return pltpu.CompilerParams(dimension_semantics=sem, vmem_limit_bytes=VMEM_LIMIT)


def _pick_tile(total, target):
    best = 8
    for cand in range(8, min(total, target) + 1, 8):
        if total % cand == 0:
            best = cand
    return best


def _adaln_kernel(s_ref, w_ref, b_ref, o_ref):
    s = s_ref[...]
    s = s * jax.nn.sigmoid(s)
    o_ref[0] = jnp.dot(s, w_ref[0], precision=lax.Precision.HIGHEST,
                       preferred_element_type=F32) + b_ref[0]


def _adaln(cond, ada_w, ada_b):
    depth, d, six_d = ada_w.shape
    tn = six_d // 4
    return pl.pallas_call(
        _adaln_kernel,
        grid=(depth, six_d // tn),
        in_specs=[pl.BlockSpec((MOD_ROWS, d), lambda l, n: (0, 0)),
                  pl.BlockSpec((1, d, tn), lambda l, n: (l, 0, n)),
                  pl.BlockSpec((1, 1, tn), lambda l, n: (l, 0, n))],
        out_specs=pl.BlockSpec((1, MOD_ROWS, tn), lambda l, n: (l, 0, n)),
        out_shape=jax.ShapeDtypeStruct((depth, MOD_ROWS, six_d), F32),
        compiler_params=_params(("arbitrary", "arbitrary")),
        name="adaln",
    )(cond, ada_w, ada_b.reshape(depth, 1, six_d))


def _is_ctx_rows(t, tt, n_lat):
    rows = t * tt + lax.broadcasted_iota(jnp.int32, (tt, 1), 0)
    return rows >= n_lat


def _mod_pick(is_ctx, modl_ref, modc_ref, idx, d):
    return jnp.where(is_ctx, modc_ref[:, idx * d:(idx + 1) * d], modl_ref[:, idx * d:(idx + 1) * d])


def _norm_mod(x, g, is_ctx, modl_ref, modc_ref, shift_idx):
    d = x.shape[-1]
    r = lax.rsqrt(jnp.mean(x * x, axis=-1, keepdims=True) + NORM_EPS)
    shift = _mod_pick(is_ctx, modl_ref, modc_ref, shift_idx, d)
    scale = _mod_pick(is_ctx, modl_ref, modc_ref, shift_idx + 1, d)
    return (x * r * g) * (1.0 + scale) + shift


def _rope(xs, cos, sin_a, sin_b):
    return xs * cos + pltpu.roll(xs, LANES - 16, 1) * sin_a + pltpu.roll(xs, 16, 1) * sin_b


def _rope_tables(n_lat, n_ctx):
    rows = n_lat // GRID_W
    r = jnp.broadcast_to(jnp.arange(rows, dtype=F32)[:, None], (rows, GRID_W)).reshape(-1)
    col = jnp.broadcast_to(jnp.arange(GRID_W, dtype=F32)[None, :], (rows, GRID_W)).reshape(-1)
    quarter = HEAD_DIM // 4
    inv_freq = ROPE_BASE ** (-jnp.arange(quarter, dtype=F32) / quarter)
    ang_r = r[:, None] * inv_freq
    ang_c = col[:, None] * inv_freq
    zeros = jnp.zeros_like(ang_r)
    cos64 = jnp.concatenate([jnp.cos(ang_r), jnp.cos(ang_r), jnp.cos(ang_c), jnp.cos(ang_c)], axis=1)
    sin_a64 = jnp.concatenate([-jnp.sin(ang_r), zeros, -jnp.sin(ang_c), zeros], axis=1)
    sin_b64 = jnp.concatenate([zeros, jnp.sin(ang_r), zeros, jnp.sin(ang_c)], axis=1)

    def full(tab, ctx_val):
        tab = jnp.concatenate([tab, jnp.full((n_ctx, HEAD_DIM), ctx_val, F32)], axis=0)
        return jnp.tile(tab, (1, LANES // HEAD_DIM))

    return full(cos64, 1.0), full(sin_a64, 0.0), full(sin_b64, 0.0)


def _head_sum_matrix():
    i = jnp.arange(LANES) // HEAD_DIM
    return (i[:, None] == i[None, :]).astype(BF16)


def _mod_specs(layer, n_batch, six_d, nargs):
    if nargs == 2:
        lat = lambda b, t: (layer * MOD_ROWS + b, 0, 0)
        ctx = lambda b, t: (layer * MOD_ROWS + n_batch, 0, 0)
    elif nargs == 3:
        lat = lambda b, t, f: (layer * MOD_ROWS + b, 0, 0)
        ctx = lambda b, t, f: (layer * MOD_ROWS + n_batch, 0, 0)
    else:
        lat = lambda b, t, e, f: (layer * MOD_ROWS + b, 0, 0)
        ctx = lambda b, t, e, f: (layer * MOD_ROWS + n_batch, 0, 0)
    return [pl.BlockSpec((None, 1, six_d), lat), pl.BlockSpec((None, 1, six_d), ctx)]


def _even_in_kernel(x_ref, modl_ref, modc_ref, g_ref, w_ref, qg_ref, kg_ref, cos_ref, sa_ref, sb_ref,
                    hs_ref, ag_ref, q_ref, k_ref, v_ref, *, tt, n_lat):
    d = x_ref.shape[-1]
    conv2 = d
    qw = d // 2
    kvw = (w_ref.shape[1] - conv2 - qw) // 2
    is_ctx = _is_ctx_rows(pl.program_id(1), tt, n_lat)
    h = _norm_mod(x_ref[0], g_ref[...], is_ctx, modl_ref, modc_ref, 0).astype(BF16)
    half = conv2 // 2
    for c in range(2):
        ag_ref[0, :, c * half:(c + 1) * half] = jnp.dot(
            h, w_ref[:, c * half:(c + 1) * half], preferred_element_type=F32)
    cos, sa, sb = cos_ref[...], sa_ref[...], sb_ref[...]
    hs = hs_ref[...]

    def head_norm_rope(ys, gain):
        ss = jnp.dot((ys * ys).astype(BF16), hs, preferred_element_type=F32)
        yn = ys * lax.rsqrt(ss * (1.0 / HEAD_DIM) + NORM_EPS) * gain
        return _rope(yn, cos, sa, sb).astype(BF16)

    q = jnp.dot(h, w_ref[:, conv2:conv2 + qw], preferred_element_type=F32)
    for c in range(qw // LANES):
        q_ref[0, :, c * LANES:(c + 1) * LANES] = head_norm_rope(q[:, c * LANES:(c + 1) * LANES], qg_ref[...])
    kv = jnp.dot(h, w_ref[:, conv2 + qw:], preferred_element_type=F32)
    for c in range(kvw // LANES):
        k_ref[0, :, c * LANES:(c + 1) * LANES] = head_norm_rope(kv[:, c * LANES:(c + 1) * LANES], kg_ref[...])
    v_ref[0] = kv[:, kvw:].astype(BF16)


def _even_in(xs, mod3, layer, g1, w_in, qg, kg, tabs, hs, n_lat):
    n_batch, t_all, d = xs.shape
    tt = _pick_tile(t_all, 768)
    qw = d // 2
    kvw = (w_in.shape[1] - d - qw) // 2
    full2 = lambda b, t: (0, 0)
    tok = lambda b, t: (b, t, 0)
    tab_spec = pl.BlockSpec((tt, LANES), lambda b, t: (t, 0))
    return pl.pallas_call(
        functools.partial(_even_in_kernel, tt=tt, n_lat=n_lat),
        grid=(n_batch, t_all // tt),
        in_specs=[pl.BlockSpec((1, tt, d), tok)] + _mod_specs(layer, n_batch, 6 * d, 2) + [
            pl.BlockSpec((1, d), full2),
            pl.BlockSpec(w_in.shape, full2),
            pl.BlockSpec((1, LANES), full2), pl.BlockSpec((1, LANES), full2),
            tab_spec, tab_spec, tab_spec,
            pl.BlockSpec((LANES, LANES), full2)],
        out_specs=[pl.BlockSpec((1, tt, d), tok), pl.BlockSpec((1, tt, qw), tok),
                   pl.BlockSpec((1, tt, kvw), tok), pl.BlockSpec((1, tt, kvw), tok)],
        out_shape=[jax.ShapeDtypeStruct((n_batch, t_all, d), F32),
                   jax.ShapeDtypeStruct((n_batch, t_all, qw), BF16),
                   jax.ShapeDtypeStruct((n_batch, t_all, kvw), BF16),
                   jax.ShapeDtypeStruct((n_batch, t_all, kvw), BF16)],
        compiler_params=_params(("arbitrary", "arbitrary")),
        name="even_in",
    )(xs, mod3, mod3, g1, w_in, qg, kg, *tabs, hs)


def _gqa_kernel(q_ref, k_ref, v_ref, o_ref, k4_ref, v4_ref, *, n_lat, n_ctx):
    kv = pl.program_id(1)
    qi = pl.program_id(2)
    t_all = n_lat + n_ctx
    group = q_ref.shape[-1] // HEAD_DIM

    @pl.when(qi == 0)
    def _():
        lane = lax.broadcasted_iota(jnp.int32, (t_all, LANES), 1)
        keep = jnp.logical_xor(lane < HEAD_DIM, kv == 1)
        for src, dst in ((k_ref, k4_ref), (v_ref, v4_ref)):
            a = src[0].astype(F32)
            rep = jnp.where(keep, a, pltpu.roll(a, HEAD_DIM, 1)).astype(BF16)
            for c in range(dst.shape[-1] // LANES):
                dst[:, c * LANES:(c + 1) * LANES] = rep

    def attend(k0, nk):
        q = q_ref[0]
        head = lax.broadcasted_iota(jnp.int32, q.shape, 1) // HEAD_DIM
        qs = jnp.concatenate([jnp.where(head == g, q, jnp.zeros_like(q)) for g in range(group)], axis=0)
        s = lax.dot_general(qs, k4_ref[k0:k0 + nk, :], (((1,), (1,)), ((), ())),
                            preferred_element_type=F32)
        p = jnp.exp2(s - jnp.max(s, axis=-1, keepdims=True))
        l = jnp.sum(p, axis=-1, keepdims=True)
        o = jnp.dot(p.astype(BF16), v4_ref[k0:k0 + nk, :], preferred_element_type=F32) / l
        out = jnp.zeros(q.shape, F32)
        for g in range(group):
            out = out + jnp.where(head == g, o[g * TQ:(g + 1) * TQ], 0.0)
        o_ref[0] = out.astype(BF16)

    @pl.when(qi < n_lat // TQ)
    def _():
        attend(0, t_all)

    @pl.when(qi >= n_lat // TQ)
    def _():
        attend(n_lat, n_ctx)


def _gqa(q, k, v, n_lat, n_ctx):
    n_batch, t_all, qw = q.shape
    kvw = k.shape[-1]
    n_kv = kvw // HEAD_DIM
    gw = qw // n_kv
    return pl.pallas_call(
        functools.partial(_gqa_kernel, n_lat=n_lat, n_ctx=n_ctx),
        grid=(n_batch, n_kv, t_all // TQ),
        in_specs=[pl.BlockSpec((1, TQ, gw), lambda b, h, i: (b, i, h)),
                  pl.BlockSpec((1, t_all, kvw), lambda b, h, i: (b, 0, 0)),
                  pl.BlockSpec((1, t_all, kvw), lambda b, h, i: (b, 0, 0))],
        out_specs=pl.BlockSpec((1, TQ, gw), lambda b, h, i: (b, i, h)),
        out_shape=jax.ShapeDtypeStruct((n_batch, t_all, qw), BF16),
        scratch_shapes=[pltpu.VMEM((t_all, gw), BF16), pltpu.VMEM((t_all, gw), BF16)],
        compiler_params=_params(("arbitrary", "arbitrary", "arbitrary")),
        name="gqa",
    )(q, k, v)


def _conv_kernel(ag_ref, w_ref, lng_ref, lnb_ref, o_ref, ypad_ref, *, n_lat, n_ctx):
    ch = o_ref.shape[-1]
    pad = CONV_WIDTH // 2
    halo = 16
    lat0 = halo
    ctx0 = 2 * halo + n_lat
    zeros = jnp.zeros((halo, ch), F32)
    ypad_ref[0:halo, :] = zeros
    ypad_ref[halo + n_lat:ctx0, :] = zeros
    ypad_ref[ctx0 + n_ctx:ctx0 + n_ctx + halo, :] = zeros

    def glu(src0, dst0, n):
        def body(i, carry):
            r = pl.multiple_of(i * CONV_CHUNK, CONV_CHUNK)
            a = ag_ref[0, pl.ds(src0 + r, CONV_CHUNK), 0:ch]
            g = ag_ref[0, pl.ds(src0 + r, CONV_CHUNK), ch:2 * ch]
            ypad_ref[pl.ds(dst0 + r, CONV_CHUNK), :] = a * jax.nn.sigmoid(g)
            return carry
        lax.fori_loop(0, n // CONV_CHUNK, body, 0)

    glu(0, lat0, n_lat)
    glu(n_lat, ctx0, n_ctx)

    def conv(src0, dst0, n):
        def body(i, carry):
            r = pl.multiple_of(i * CONV_CHUNK, CONV_CHUNK)
            accs = []
            for c in range(ch // LANES):
                cols = slice(c * LANES, (c + 1) * LANES)
                win = ypad_ref[pl.ds(src0 - halo + r, CONV_CHUNK + 2 * halo), cols]
                acc = jnp.zeros((CONV_CHUNK, LANES), F32)
                for s in range(8):
                    taps = [j for j in range(CONV_WIDTH) if (j + halo - pad) % 8 == s]
                    if not taps:
                        continue
                    sh = win[s:s + CONV_CHUNK + 2 * halo - 8]
                    for j in taps:
                        a8 = (j + halo - pad) - s
                        acc = acc + w_ref[j:j + 1, cols] * sh[a8:a8 + CONV_CHUNK]
                accs.append(acc)
            mu = sum(jnp.sum(a, axis=-1, keepdims=True) for a in accs) * (1.0 / ch)
            cens = [a - mu for a in accs]
            var = sum(jnp.sum(cn * cn, axis=-1, keepdims=True) for cn in cens) * (1.0 / ch)
            rstd = lax.rsqrt(var + LN_EPS)
            for c, cn in enumerate(cens):
                cols = slice(c * LANES, (c + 1) * LANES)
                yn = cn * rstd * lng_ref[:, cols] + lnb_ref[:, cols]
                o_ref[0, pl.ds(dst0 + r, CONV_CHUNK), cols] = (yn * jax.nn.sigmoid(yn)).astype(BF16)
            return carry
        lax.fori_loop(0, n // CONV_CHUNK, body, 0)

    conv(lat0, 0, n_lat)
    conv(ctx0, n_lat, n_ctx)


def _conv(ag, conv_w, ln_g, ln_b, n_lat, n_ctx):
    n_batch, t_all, two_ch = ag.shape
    ch = two_ch // 2
    full2 = lambda b: (0, 0)
    return pl.pallas_call(
        functools.partial(_conv_kernel, n_lat=n_lat, n_ctx=n_ctx),
        grid=(n_batch,),
        in_specs=[pl.BlockSpec((1, t_all, two_ch), lambda b: (b, 0, 0)),
                  pl.BlockSpec(conv_w.shape, full2),
                  pl.BlockSpec((1, ch), full2), pl.BlockSpec((1, ch), full2)],
        out_specs=pl.BlockSpec((1, t_all, ch), lambda b: (b, 0, 0)),
        out_shape=jax.ShapeDtypeStruct((n_batch, t_all, ch), BF16),
        scratch_shapes=[pltpu.VMEM((t_all + 48, ch), F32)],
        compiler_params=_params(("arbitrary",)),
        name="conv",
    )(ag, conv_w, ln_g, ln_b)


def _odd_in_kernel(x_ref, modl_ref, modc_ref, g_ref, w_ref, cos_ref, sa_ref, sb_ref,
                   q_ref, k_ref, v_ref, *, tt, n_lat):
    d = x_ref.shape[-1]
    is_ctx = _is_ctx_rows(pl.program_id(1), tt, n_lat)
    h = _norm_mod(x_ref[0], g_ref[...], is_ctx, modl_ref, modc_ref, 0).astype(BF16)
    cos, sa, sb = cos_ref[...], sa_ref[...], sb_ref[...]
    cw = 512
    for part, (dst, scale) in enumerate(((q_ref, ATT_SCALE), (k_ref, None), (v_ref, None))):
        for c0 in range(0, d, cw):
            y = jnp.dot(h, w_ref[:, part * d + c0:part * d + c0 + cw], preferred_element_type=F32)
            if part == 2:
                dst[0, :, c0:c0 + cw] = y.astype(BF16)
                continue
            for c in range(cw // LANES):
                ys = _rope(y[:, c * LANES:(c + 1) * LANES], cos, sa, sb)
                if scale is not None:
                    ys = ys * scale
                dst[0, :, c0 + c * LANES:c0 + (c + 1) * LANES] = ys.astype(BF16)


def _odd_in(xs, mod3, layer, g1, w_in, tabs, n_lat):
    n_batch, t_all, d = xs.shape
    tt = _pick_tile(t_all, 768)
    full2 = lambda b, t: (0, 0)
    tok = lambda b, t: (b, t, 0)
    tab_spec = pl.BlockSpec((tt, LANES), lambda b, t: (t, 0))
    return pl.pallas_call(
        functools.partial(_odd_in_kernel, tt=tt, n_lat=n_lat),
        grid=(n_batch, t_all // tt),
        in_specs=[pl.BlockSpec((1, tt, d), tok)] + _mod_specs(layer, n_batch, 6 * d, 2) + [
            pl.BlockSpec((1, d), full2),
            pl.BlockSpec(w_in.shape, full2),
            tab_spec, tab_spec, tab_spec],
        out_specs=[pl.BlockSpec((1, tt, d), tok)] * 3,
        out_shape=[jax.ShapeDtypeStruct((n_batch, t_all, d), BF16)] * 3,
        compiler_params=_params(("arbitrary", "arbitrary")),
        name="odd_in",
    )(xs, mod3, mod3, g1, w_in, *tabs)


def _diff_kernel(q_ref, k_ref, v_ref, lam_ref, sg_ref, o_ref, *, n_lat, n_ctx, lam_init):
    qi = pl.program_id(2)
    t_all = n_lat + n_ctx
    lp = lam_ref[...]
    lam = (jnp.exp(jnp.sum(lp[0:1] * lp[1:2], axis=-1, keepdims=True))
           - jnp.exp(jnp.sum(lp[2:3] * lp[3:4], axis=-1, keepdims=True)) + lam_init)

    def attend(k0, nk):
        q = q_ref[0]
        first = lax.broadcasted_iota(jnp.int32, q.shape, 1) < HEAD_DIM
        zero = jnp.zeros_like(q)
        qs = jnp.concatenate([jnp.where(first, q, zero), jnp.where(first, zero, q)], axis=0)
        s = lax.dot_general(qs, k_ref[0, k0:k0 + nk, :], (((1,), (1,)), ((), ())),
                            preferred_element_type=F32)
        p = jnp.exp2(s - jnp.max(s, axis=-1, keepdims=True))
        p = p / jnp.sum(p, axis=-1, keepdims=True)
        a = p[0:TQ] - lam * p[TQ:2 * TQ]
        o = jnp.dot(a.astype(BF16), v_ref[0, k0:k0 + nk, :], preferred_element_type=F32)
        r = lax.rsqrt(jnp.mean(o * o, axis=-1, keepdims=True) + NORM_EPS)
        o_ref[0] = ((o * r * sg_ref[...]) * (1.0 - lam_init)).astype(BF16)

    @pl.when(qi < n_lat // TQ)
    def _():
        attend(0, t_all)

    @pl.when(qi >= n_lat // TQ)
    def _():
        attend(n_lat, n_ctx)


def _diff_attn(q, k, v, lam_p, subln_g, lam_init, n_lat, n_ctx):
    n_batch, t_all, d = q.shape
    hw = 2 * HEAD_DIM
    full2 = lambda b, h, i: (0, 0)
    return pl.pallas_call(
        functools.partial(_diff_kernel, n_lat=n_lat, n_ctx=n_ctx, lam_init=lam_init),
        grid=(n_batch, d // hw, t_all // TQ),
        in_specs=[pl.BlockSpec((1, TQ, hw), lambda b, h, i: (b, i, h)),
                  pl.BlockSpec((1, t_all, hw), lambda b, h, i: (b, 0, h)),
                  pl.BlockSpec((1, t_all, hw), lambda b, h, i: (b, 0, h)),
                  pl.BlockSpec(lam_p.shape, full2),
                  pl.BlockSpec((1, hw), full2)],
        out_specs=pl.BlockSpec((1, TQ, hw), lambda b, h, i: (b, i, h)),
        out_shape=jax.ShapeDtypeStruct((n_batch, t_all, d), BF16),
        compiler_params=_params(("arbitrary", "arbitrary", "arbitrary")),
        name="diff_attn",
    )(q, k, v, lam_p, subln_g)


def _out_kernel(*refs, tt, n_lat, n_parts):
    x_ref, modl_ref, modc_ref = refs[0:3]
    parts = refs[3:3 + n_parts]
    w_ref = refs[3 + n_parts]
    o_ref = refs[4 + n_parts]
    d = x_ref.shape[-1]
    is_ctx = _is_ctx_rows(pl.program_id(1), tt, n_lat)
    gate = _mod_pick(is_ctx, modl_ref, modc_ref, 2, d)
    y = jnp.zeros((tt, d), F32)
    row = 0
    for p in parts:
        kp = p.shape[-1]
        y = y + jnp.dot(p[0], w_ref[row:row + kp, :], preferred_element_type=F32)
        row += kp
    o_ref[0] = x_ref[0] + gate * y


def _out_proj(xs, mod3, layer, parts, w_out, n_lat):
    n_batch, t_all, d = xs.shape
    tt = _pick_tile(t_all, 768)
    tok = lambda b, t: (b, t, 0)
    return pl.pallas_call(
        functools.partial(_out_kernel, tt=tt, n_lat=n_lat, n_parts=len(parts)),
        grid=(n_batch, t_all // tt),
        in_specs=[pl.BlockSpec((1, tt, d), tok)] + _mod_specs(layer, n_batch, 6 * d, 2)
        + [pl.BlockSpec((1, tt, p.shape[-1]), tok) for p in parts]
        + [pl.BlockSpec(w_out.shape, lambda b, t: (0, 0))],
        out_specs=pl.BlockSpec((1, tt, d), tok),
        out_shape=jax.ShapeDtypeStruct(xs.shape, F32),
        compiler_params=_params(("arbitrary", "arbitrary")),
        name="out_proj",
    )(xs, mod3, mod3, *parts, w_out)


def _ffn_kernel(x_ref, modl_ref, modc_ref, g_ref, wg_ref, wu_ref, wd_ref, o_ref, h_ref, acc_ref, *, tt, n_lat):
    f = pl.program_id(2)
    d = x_ref.shape[-1]
    is_ctx = _is_ctx_rows(pl.program_id(1), tt, n_lat)

    @pl.when(f == 0)
    def _():
        h_ref[...] = _norm_mod(x_ref[0], g_ref[...], is_ctx, modl_ref, modc_ref, 3).astype(BF16)
        acc_ref[...] = jnp.zeros_like(acc_ref)

    h = h_ref[...]
    g = jnp.dot(h, wg_ref[...], preferred_element_type=F32)
    u = jnp.dot(h, wu_ref[...], preferred_element_type=F32)
    act = (g * jax.nn.sigmoid(g) * u).astype(BF16)
    acc_ref[...] += jnp.dot(act, wd_ref[...], preferred_element_type=F32)

    @pl.when(f == pl.num_programs(2) - 1)
    def _():
        gate = _mod_pick(is_ctx, modl_ref, modc_ref, 5, d)
        o_ref[0] = x_ref[0] + gate * acc_ref[...]


def _ffn(xs, mod3, layer, g2, wg, wu, wd, n_lat):
    n_batch, t_all, d = xs.shape
    ffn = wg.shape[-1]
    tt = _pick_tile(t_all, 768)
    tf = 512
    tok = lambda b, t, f: (b, t, 0)
    return pl.pallas_call(
        functools.partial(_ffn_kernel, tt=tt, n_lat=n_lat),
        grid=(n_batch, t_all // tt, ffn // tf),
        in_specs=[pl.BlockSpec((1, tt, d), tok)] + _mod_specs(layer, n_batch, 6 * d, 3) + [
            pl.BlockSpec((1, d), lambda b, t, f: (0, 0)),
            pl.BlockSpec((d, tf), lambda b, t, f: (0, f)),
            pl.BlockSpec((d, tf), lambda b, t, f: (0, f)),
            pl.BlockSpec((tf, d), lambda b, t, f: (f, 0))],
        out_specs=pl.BlockSpec((1, tt, d), tok),
        out_shape=jax.ShapeDtypeStruct(xs.shape, F32),
        scratch_shapes=[pltpu.VMEM((tt, d), BF16), pltpu.VMEM((tt, d), F32)],
        compiler_params=_params(("arbitrary", "arbitrary", "arbitrary")),
        name="ffn",
    )(xs, mod3, mod3, g2, wg, wu, wd)


def _top2(logits):
    lane = lax.broadcasted_iota(jnp.int32, logits.shape, 1).astype(F32)
    neg = jnp.float32(-jnp.inf)
    lg = jnp.where(lane < N_EXPERTS, logits, neg)
    m1 = jnp.max(lg, axis=-1, keepdims=True)
    i1 = jnp.min(jnp.where(lg == m1, lane, float(LANES)), axis=-1, keepdims=True)
    lg2 = jnp.where(lane == i1, neg, lg)
    m2 = jnp.max(lg2, axis=-1, keepdims=True)
    i2 = jnp.min(jnp.where(lg2 == m2, lane, float(LANES)), axis=-1, keepdims=True)
    e2 = jnp.exp(m2 - m1)
    den = 1.0 + e2
    return lane, i1, i2, 1.0 / den, e2 / den


def _route_kernel(x_ref, modl_ref, modc_ref, g_ref, rw_ref, info_ref, w_ref, cnt_ref, run_ref, *, tt, n_lat):
    @pl.when((pl.program_id(0) == 0) & (pl.program_id(1) == 0))
    def _():
        run_ref[...] = jnp.zeros_like(run_ref)

    is_ctx = _is_ctx_rows(pl.program_id(1), tt, n_lat)
    h = _norm_mod(x_ref[0], g_ref[...], is_ctx, modl_ref, modc_ref, 3)
    logits = jnp.dot(h, rw_ref[...], precision=lax.Precision.HIGHEST, preferred_element_type=F32)
    lane, i1, i2, w1, w2 = _top2(logits)
    chosen = jnp.where((lane == i1) | (lane == i2), 1.0, 0.0)
    before = (lax.broadcasted_iota(jnp.int32, (tt, tt), 1) < lax.broadcasted_iota(jnp.int32, (tt, tt), 0))
    pre = jnp.dot(jnp.where(before, 1.0, 0.0).astype(BF16), chosen.astype(BF16),
                  preferred_element_type=F32) + run_ref[...]
    r1 = jnp.sum(jnp.where(lane == i1, pre, 0.0), axis=-1, keepdims=True)
    r2 = jnp.sum(jnp.where(lane == i2, pre, 0.0), axis=-1, keepdims=True)
    run_ref[...] += jnp.sum(chosen, axis=0, keepdims=True)
    cnt_ref[...] = run_ref[...]
    info = jnp.where(lane == 0.0, i1, jnp.where(lane == 1.0, i2, jnp.where(lane == 2.0, r1,
                                                                          jnp.where(lane == 3.0, r2, 0.0))))
    info_ref[0] = info.T[0:8].astype(jnp.int32)
    w_ref[0] = jnp.where(lane == 0.0, w1, jnp.where(lane == 1.0, w2, 0.0))


def _route(xs, mod3, layer, g2, router_w, n_lat, nb, tt):
    n_batch, t_all, d = xs.shape
    rw = jnp.zeros((d, LANES), F32).at[:, :router_w.shape[1]].set(router_w)
    return pl.pallas_call(
        functools.partial(_route_kernel, tt=tt, n_lat=n_lat),
        grid=(n_batch, nb),
        in_specs=[pl.BlockSpec((1, tt, d), lambda b, j: (b, j, 0))] + _mod_specs(layer, n_batch, 6 * d, 2) + [
            pl.BlockSpec((1, d), lambda b, j: (0, 0)),
            pl.BlockSpec((d, LANES), lambda b, j: (0, 0))],
        out_specs=[pl.BlockSpec((1, 8, tt), lambda b, j: (b * nb + j, 0, 0)),
                   pl.BlockSpec((1, tt, LANES), lambda b, j: (b, j, 0)),
                   pl.BlockSpec((1, LANES), lambda b, j: (0, 0))],
        out_shape=[jax.ShapeDtypeStruct((n_batch * nb, 8, tt), jnp.int32),
                   jax.ShapeDtypeStruct((n_batch, nb * tt, LANES), F32),
                   jax.ShapeDtypeStruct((1, LANES), F32)],
        scratch_shapes=[pltpu.VMEM((1, LANES), F32)],
        compiler_params=_params(("arbitrary", "arbitrary")),
        name="moe_route",
    )(xs, mod3, mod3, g2, rw)


def _slot_row(starts_ref, info_ref, k, r, p):
    slot = starts_ref[info_ref[k, r]] + info_ref[2 + k, r]
    return pl.multiple_of(slot * p, p)


def _dispatch_kernel(starts_ref, fill_ref, nu_ref, x_ref, modl_ref, modc_ref, g_ref, info_ref, xs_ref,
                     hbuf_ref, zero_ref, sems, zsem, *, tt, n_lat, tm):
    nb = pl.num_programs(1)
    step = pl.program_id(0) * nb + pl.program_id(1)
    nsteps = pl.num_programs(0) * nb
    slot = lax.rem(step, 2)
    p = x_ref.shape[-1] // LANES

    def fill_copy(e):
        return pltpu.make_async_copy(zero_ref, xs_ref.at[pl.ds(pl.multiple_of(fill_ref[e] * p, p), tm * p)], zsem)

    @pl.when(step == 0)
    def _():
        zero_ref[...] = jnp.zeros_like(zero_ref)
        for e in range(N_EXPERTS):
            fill_copy(e).start()
        for e in range(N_EXPERTS):
            fill_copy(e).wait()

        def clear_tile(t, carry):
            cp = pltpu.make_async_copy(
                zero_ref, xs_ref.at[pl.ds(pl.multiple_of(t * (tm * p), tm * p), tm * p)], zsem)
            cp.start()
            cp.wait()
            return carry

        lax.fori_loop(nu_ref[0], xs_ref.shape[0] // (tm * p), clear_tile, 0)

    is_ctx = _is_ctx_rows(pl.program_id(1), tt, n_lat)
    h = _norm_mod(x_ref[0], g_ref[...], is_ctx, modl_ref, modc_ref, 3)
    for s in range(p):
        hbuf_ref[slot, pl.ds(s, tt, stride=p), :] = h[:, s * LANES:(s + 1) * LANES]

    def issue(r, carry):
        src = hbuf_ref.at[slot, pl.ds(pl.multiple_of(r * p, p), p)]
        for k in range(2):
            pltpu.make_async_copy(src, xs_ref.at[pl.ds(_slot_row(starts_ref, info_ref, k, r, p), p)],
                                  sems.at[slot]).start()
        return carry

    lax.fori_loop(0, tt, issue, 0, unroll=8)

    def wait_rows(sl):
        for _ in range(2):
            pltpu.make_async_copy(hbuf_ref.at[sl], xs_ref.at[pl.ds(0, tt * p)], sems.at[sl]).wait()

    @pl.when(step > 0)
    def _():
        wait_rows(1 - slot)

    @pl.when(step == nsteps - 1)
    def _():
        wait_rows(slot)


def _dispatch(xs, mod3, layer, g2, info, starts, fill, n_used, n_lat, nb, tt, tm, n_rows):
    n_batch, _, d = xs.shape
    p = d // LANES
    grid_spec = pltpu.PrefetchScalarGridSpec(
        num_scalar_prefetch=3,
        grid=(n_batch, nb),
        in_specs=[pl.BlockSpec((1, tt, d), lambda b, j, *_: (b, j, 0)),
                  pl.BlockSpec((None, 1, 6 * d), lambda b, j, *_: (layer * MOD_ROWS + b, 0, 0)),
                  pl.BlockSpec((None, 1, 6 * d), lambda b, j, *_: (layer * MOD_ROWS + n_batch, 0, 0)),
                  pl.BlockSpec((1, d), lambda b, j, *_: (0, 0)),
                  pl.BlockSpec((None, 8, tt), lambda b, j, *_: (b * nb + j, 0, 0), memory_space=pltpu.SMEM)],
        out_specs=pl.BlockSpec(memory_space=pl.ANY),
        scratch_shapes=[pltpu.VMEM((2, tt * p, LANES), F32), pltpu.VMEM((tm * p, LANES), F32),
                        pltpu.SemaphoreType.DMA((2,)), pltpu.SemaphoreType.DMA(())],
    )
    return pl.pallas_call(
        functools.partial(_dispatch_kernel, tt=tt, n_lat=n_lat, tm=tm),
        grid_spec=grid_spec,
        out_shape=jax.ShapeDtypeStruct((n_rows * p, LANES), F32),
        compiler_params=_params(("arbitrary", "arbitrary")),
        name="moe_dispatch",
    )(starts, fill, n_used, xs, mod3, mod3, g2, info)


def _experts_kernel(te_ref, nu_ref, xs_ref, wg_ref, wu_ref, wd_ref, o_ref, h_ref, acc_ref, *, tm):
    t = pl.program_id(0)
    f = pl.program_id(1)
    p = o_ref.shape[0] // tm
    live = t < nu_ref[0]

    @pl.when(live & (f == 0))
    def _():
        h_ref[...] = jnp.concatenate([xs_ref[pl.ds(s, tm, stride=p), :] for s in range(p)],
                                     axis=-1).astype(BF16)
        acc_ref[...] = jnp.zeros_like(acc_ref)

    @pl.when(live)
    def _():
        h = h_ref[...]
        g = jnp.dot(h, wg_ref[0], preferred_element_type=F32)
        u = jnp.dot(h, wu_ref[0], preferred_element_type=F32)
        act = (g * jax.nn.sigmoid(g) * u).astype(BF16)
        acc_ref[...] += jnp.dot(act, wd_ref[0], preferred_element_type=F32)

    @pl.when(live & (f == pl.num_programs(1) - 1))
    def _():
        for s in range(p):
            o_ref[pl.ds(s, tm, stride=p), :] = acc_ref[:, s * LANES:(s + 1) * LANES]

    @pl.when(jnp.logical_not(live) & (f == pl.num_programs(1) - 1))
    def _():
        o_ref[...] = jnp.zeros_like(o_ref)


def _experts(xsort, tile_expert, n_used, wg, wu, wd, tm, n_tiles):
    n_exp, d, ffn = wg.shape
    p = d // LANES
    tf = 512
    nf = ffn // tf

    def fcol(t, f, te, nu):
        return jnp.where(t < nu[0], f, nf - 1)

    grid_spec = pltpu.PrefetchScalarGridSpec(
        num_scalar_prefetch=2,
        grid=(n_tiles, nf),
        in_specs=[pl.BlockSpec((tm * p, LANES), lambda t, f, te, nu: (t, 0)),
                  pl.BlockSpec((1, d, tf), lambda t, f, te, nu: (te[t], 0, fcol(t, f, te, nu))),
                  pl.BlockSpec((1, d, tf), lambda t, f, te, nu: (te[t], 0, fcol(t, f, te, nu))),
                  pl.BlockSpec((1, tf, d), lambda t, f, te, nu: (te[t], fcol(t, f, te, nu), 0))],
        out_specs=pl.BlockSpec((tm * p, LANES), lambda t, f, te, nu: (t, 0)),
        scratch_shapes=[pltpu.VMEM((tm, d), BF16), pltpu.VMEM((tm, d), F32)],
    )
    return pl.pallas_call(
        functools.partial(_experts_kernel, tm=tm),
        grid_spec=grid_spec,
        out_shape=jax.ShapeDtypeStruct((n_tiles * tm * p, LANES), F32),
        compiler_params=_params(("arbitrary", "arbitrary")),
        name="moe_experts",
    )(tile_expert, n_used, xsort, wg, wu, wd)


def _combine_kernel(starts_ref, x_ref, modl_ref, modc_ref, w_ref, info0_ref, infon_ref, y_ref, fg_ref,
                    o_ref, buf1_ref, buf2_ref, sems, *, tt, n_lat, final):
    nb = pl.num_programs(1)
    step = pl.program_id(0) * nb + pl.program_id(1)
    nsteps = pl.num_programs(0) * nb
    slot = lax.rem(step, 2)
    d = x_ref.shape[-1]
    p = d // LANES
    bufs = (buf1_ref, buf2_ref)

    def issue_block(info_ref, sl):
        def issue(r, carry):
            for k in range(2):
                pltpu.make_async_copy(y_ref.at[pl.ds(_slot_row(starts_ref, info_ref, k, r, p), p)],
                                      bufs[k].at[sl, pl.ds(pl.multiple_of(r * p, p), p)], sems.at[sl]).start()
            return carry
        lax.fori_loop(0, tt, issue, 0, unroll=8)

    @pl.when(step == 0)
    def _():
        issue_block(info0_ref, 0)

    @pl.when(step + 1 < nsteps)
    def _():
        issue_block(infon_ref, 1 - slot)

    for k in range(2):
        pltpu.make_async_copy(y_ref.at[pl.ds(0, tt * p)], bufs[k].at[slot], sems.at[slot]).wait()

    is_ctx = _is_ctx_rows(pl.program_id(1), tt, n_lat)
    w = w_ref[0]
    w1, w2 = w[:, 0:1], w[:, 1:2]
    outs = []
    for s in range(p):
        cols = slice(s * LANES, (s + 1) * LANES)
        y = w1 * buf1_ref[slot, pl.ds(s, tt, stride=p), :] + w2 * buf2_ref[slot, pl.ds(s, tt, stride=p), :]
        gate = jnp.where(is_ctx, modc_ref[:, 5 * d + s * LANES:5 * d + (s + 1) * LANES],
                         modl_ref[:, 5 * d + s * LANES:5 * d + (s + 1) * LANES])
        xn = x_ref[0, :, cols] + gate * y
        if final:
            outs.append(xn)
        else:
            o_ref[0, :, cols] = xn
    if final:
        ss = sum(jnp.sum(xn * xn, axis=-1, keepdims=True) for xn in outs)
        r = lax.rsqrt(ss * (1.0 / d) + NORM_EPS)
        for s, xn in enumerate(outs):
            cols = slice(s * LANES, (s + 1) * LANES)
            o_ref[0, :, cols] = xn * r * fg_ref[:, cols]


def _combine(xs, mod3, layer, wts, info, starts, y, final_g, n_lat, nb, tt, final):
    n_batch, _, d = xs.shape
    p = d // LANES
    nblk = n_batch * nb
    out_rows = nb * tt
    grid_spec = pltpu.PrefetchScalarGridSpec(
        num_scalar_prefetch=1,
        grid=(n_batch, nb),
        in_specs=[pl.BlockSpec((1, tt, d), lambda b, j, *_: (b, j, 0)),
                  pl.BlockSpec((None, 1, 6 * d), lambda b, j, *_: (layer * MOD_ROWS + b, 0, 0)),
                  pl.BlockSpec((None, 1, 6 * d), lambda b, j, *_: (layer * MOD_ROWS + n_batch, 0, 0)),
                  pl.BlockSpec((1, tt, LANES), lambda b, j, *_: (b, j, 0)),
                  pl.BlockSpec((None, 8, tt), lambda b, j, *_: (0, 0, 0), memory_space=pltpu.SMEM),
                  pl.BlockSpec((None, 8, tt), lambda b, j, *_: (jnp.minimum(b * nb + j + 1, nblk - 1), 0, 0),
                               memory_space=pltpu.SMEM),
                  pl.BlockSpec(memory_space=pl.ANY),
                  pl.BlockSpec((1, d), lambda b, j, *_: (0, 0))],
        out_specs=pl.BlockSpec((1, tt, d), lambda b, j, *_: (b, j, 0)),
        scratch_shapes=[pltpu.VMEM((2, tt * p, LANES), F32), pltpu.VMEM((2, tt * p, LANES), F32),
                        pltpu.SemaphoreType.DMA((2,))],
    )
    return pl.pallas_call(
        functools.partial(_combine_kernel, tt=tt, n_lat=n_lat, final=final),
        grid_spec=grid_spec,
        out_shape=jax.ShapeDtypeStruct((n_batch, out_rows, d), F32),
        compiler_params=_params(("arbitrary", "arbitrary")),
        name="moe_combine",
    )(starts, xs, mod3, mod3, wts, info, info, y, final_g)


def _moe(xs, mod3, layer, g2, router_w, wg, wu, wd, final_g, n_lat, final):
    n_batch, t_all, d = xs.shape
    n_exp = wg.shape[0]
    tt = TQ
    tm = 512
    nb = (n_lat if final else t_all) // tt
    n_assign = 2 * n_batch * nb * tt
    n_tiles = -(-n_assign // tm) + n_exp
    info, wts, cnt = _route(xs, mod3, layer, g2, router_w, n_lat, nb, tt)
    counts = cnt[0, :n_exp].astype(jnp.int32)
    tiles_per = (counts + tm - 1) // tm
    tile_end = jnp.cumsum(tiles_per)
    starts = (tile_end - tiles_per) * tm
    n_used = tile_end[-1:]
    tile_ids = jnp.arange(n_tiles, dtype=jnp.int32)
    tile_expert = jnp.minimum(jnp.sum(tile_ids[:, None] >= tile_end[None, :], axis=1), n_exp - 1).astype(jnp.int32)
    xsort = _dispatch(xs, mod3, layer, g2, info, starts, starts + counts, n_used, n_lat, nb, tt, tm,
                      (n_tiles + 1) * tm)
    y = _experts(xsort, tile_expert, n_used, wg, wu, wd, tm, n_tiles)
    return _combine(xs, mod3, layer, wts, info, starts, y, final_g, n_lat, nb, tt, final)


def _final_kernel(x_ref, g_ref, o_ref):
    x = x_ref[0]
    o_ref[0] = x * lax.rsqrt(jnp.mean(x * x, axis=-1, keepdims=True) + NORM_EPS) * g_ref[...]


def _final_norm(xs, g, n_lat):
    n_batch, _, d = xs.shape
    tt = _pick_tile(n_lat, 256)
    return pl.pallas_call(
        _final_kernel,
        grid=(n_batch, n_lat // tt),
        in_specs=[pl.BlockSpec((1, tt, d), lambda b, t: (b, t, 0)), pl.BlockSpec((1, d), lambda b, t: (0, 0))],
        out_specs=pl.BlockSpec((1, tt, d), lambda b, t: (b, t, 0)),
        out_shape=jax.ShapeDtypeStruct((n_batch, n_lat, d), F32),
        compiler_params=_params(("arbitrary", "arbitrary")),
        name="final_norm",
    )(xs, g)


def kernel(x, c, ctx, c_ctx, ada_w, ada_b, norm1_g, norm2_g, ev_w_in, ev_conv_w, ev_ln_g, ev_ln_b,
           ev_q_norm_g, ev_k_norm_g, ev_w_out, ev_ffn_wg, ev_ffn_wu, ev_ffn_wd, od_w_in, od_lam,
           od_subln_g, od_w_out, od_router_w, od_moe_wg, od_moe_wu, od_moe_wd, final_norm_g):
    n_batch, n_lat, d = x.shape
    n_ctx = ctx.shape[1]
    depth = ada_w.shape[0]
    assert n_batch + 1 <= MOD_ROWS and n_lat % TQ == 0 and n_ctx % TQ == 0 and d % LANES == 0

    cond = jnp.zeros((MOD_ROWS, d), F32).at[:n_batch].set(c).at[n_batch].set(c_ctx)
    mod3 = _adaln(cond, ada_w, ada_b).reshape(depth * MOD_ROWS, 1, 6 * d)
    tabs = _rope_tables(n_lat, n_ctx)
    hs = _head_sum_matrix()
    xs = jnp.concatenate([x, ctx], axis=1)

    for layer in range(depth):
        i = layer // 2
        g1 = norm1_g[layer].reshape(1, d)
        g2 = norm2_g[layer].reshape(1, d)
        if layer % 2 == 0:
            rep = LANES // HEAD_DIM
            qg = (jnp.tile(ev_q_norm_g[i], rep) * ATT_SCALE).reshape(1, LANES)
            kg = jnp.tile(ev_k_norm_g[i], rep).reshape(1, LANES)
            ag, q, k, v = _even_in(xs, mod3, layer, g1, ev_w_in[i].astype(BF16), qg, kg, tabs, hs, n_lat)
            att = _gqa(q, k, v, n_lat, n_ctx)
            cv = _conv(ag, ev_conv_w[i], ev_ln_g[i].reshape(1, -1), ev_ln_b[i].reshape(1, -1), n_lat, n_ctx)
            xs = _out_proj(xs, mod3, layer, [cv, att], ev_w_out[i].astype(BF16), n_lat)
            xs = _ffn(xs, mod3, layer, g2, ev_ffn_wg[i].astype(BF16), ev_ffn_wu[i].astype(BF16),
                      ev_ffn_wd[i].astype(BF16), n_lat)
        else:
            lam_init = 0.8 - 0.6 * math.exp(-0.3 * layer)
            q, k, v = _odd_in(xs, mod3, layer, g1, od_w_in[i].astype(BF16), tabs, n_lat)
            att = _diff_attn(q, k, v, od_lam[i], od_subln_g[i].reshape(1, -1), lam_init, n_lat, n_ctx)
            xs = _out_proj(xs, mod3, layer, [att], od_w_out[i].astype(BF16), n_lat)
            xs = _moe(xs, mod3, layer, g2, od_router_w[i], od_moe_wg[i].astype(BF16),
                      od_moe_wu[i].astype(BF16), od_moe_wd[i].astype(BF16), final_norm_g.reshape(1, d),
                      n_lat, final=layer == depth - 1)
    if depth % 2 == 0:
        return xs
    return _final_norm(xs, final_norm_g.reshape(1, d), n_lat)
```

```python
import functools
import math

import jax
import jax.numpy as jnp
from jax import lax
from jax.experimental import pallas as pl
from jax.experimental.pallas import tpu as pltpu

F32 = jnp.float32
BF16 = jnp.bfloat16

HEAD_DIM = 64
GRID_W = 64
ROPE_BASE = 10000.0
NORM_EPS = 1e-6
LN_EPS = 1e-5
CONV_WIDTH = 31
N_EXPERTS = 8
LANES = 128
MOD_ROWS = 24
VMEM_LIMIT = 56 * 1024 * 1024
ATT_SCALE = HEAD_DIM ** -0.5 * math.log2(math.e)
TQ = 256
CONV_CHUNK = 64


def _params(sem):
    return pltpu.CompilerParams(dimension_semantics=sem, vmem_limit_bytes=VMEM_LIMIT)


def _pick_tile(total, target):
    best = 8
    for cand in range(8, min(total, target) + 1, 8):
        if total % cand == 0:
            best = cand
    return best


def _adaln_kernel(s_ref, w_ref, b_ref, o_ref):
    s = s_ref[...]
    s = s * jax.nn.sigmoid(s)
    o_ref[0] = jnp.dot(s, w_ref[0], precision=lax.Precision.HIGHEST,
                       preferred_element_type=F32) + b_ref[0]


def _adaln(cond, ada_w, ada_b):
    depth, d, six_d = ada_w.shape
    tn = six_d // 4
    return pl.pallas_call(
        _adaln_kernel,
        grid=(depth, six_d // tn),
        in_specs=[pl.BlockSpec((MOD_ROWS, d), lambda l, n: (0, 0)),
                  pl.BlockSpec((1, d, tn), lambda l, n: (l, 0, n)),
                  pl.BlockSpec((1, 1, tn), lambda l, n: (l, 0, n))],
        out_specs=pl.BlockSpec((1, MOD_ROWS, tn), lambda l, n: (l, 0, n)),
        out_shape=jax.ShapeDtypeStruct((depth, MOD_ROWS, six_d), F32),
        compiler_params=_params(("arbitrary", "arbitrary")),
        name="adaln",
    )(cond, ada_w, ada_b.reshape(depth, 1, six_d))


def _is_ctx_rows(t, tt, n_lat):
    rows = t * tt + lax.broadcasted_iota(jnp.int32, (tt, 1), 0)
    return rows >= n_lat


def _mod_pick(is_ctx, modl_ref, modc_ref, idx, d):
    return jnp.where(is_ctx, modc_ref[:, idx * d:(idx + 1) * d], modl_ref[:, idx * d:(idx + 1) * d])


def _norm_mod(x, g, is_ctx, modl_ref, modc_ref, shift_idx):
    d = x.shape[-1]
    r = lax.rsqrt(jnp.mean(x * x, axis=-1, keepdims=True) + NORM_EPS)
    shift = _mod_pick(is_ctx, modl_ref, modc_ref, shift_idx, d)
    scale = _mod_pick(is_ctx, modl_ref, modc_ref, shift_idx + 1, d)
    return (x * r * g) * (1.0 + scale) + shift


def _rope(xs, cos, sin_a, sin_b):
    return xs * cos + pltpu.roll(xs, LANES - 16, 1) * sin_a + pltpu.roll(xs, 16, 1) * sin_b


def _rope_tables(n_lat, n_ctx):
    rows = n_lat // GRID_W
    r = jnp.broadcast_to(jnp.arange(rows, dtype=F32)[:, None], (rows, GRID_W)).reshape(-1)
    col = jnp.broadcast_to(jnp.arange(GRID_W, dtype=F32)[None, :], (rows, GRID_W)).reshape(-1)
    quarter = HEAD_DIM // 4
    inv_freq = ROPE_BASE ** (-jnp.arange(quarter, dtype=F32) / quarter)
    ang_r = r[:, None] * inv_freq
    ang_c = col[:, None] * inv_freq
    zeros = jnp.zeros_like(ang_r)
    cos64 = jnp.concatenate([jnp.cos(ang_r), jnp.cos(ang_r), jnp.cos(ang_c), jnp.cos(ang_c)], axis=1)
    sin_a64 = jnp.concatenate([-jnp.sin(ang_r), zeros, -jnp.sin(ang_c), zeros], axis=1)
    sin_b64 = jnp.concatenate([zeros, jnp.sin(ang_r), zeros, jnp.sin(ang_c)], axis=1)

    def full(tab, ctx_val):
        tab = jnp.concatenate([tab, jnp.full((n_ctx, HEAD_DIM), ctx_val, F32)], axis=0)
        return jnp.tile(tab, (1, LANES // HEAD_DIM))

    return full(cos64, 1.0), full(sin_a64, 0.0), full(sin_b64, 0.0)


def _head_sum_matrix():
    i = jnp.arange(LANES) // HEAD_DIM
    return (i[:, None] == i[None, :]).astype(BF16)


def _mod_specs(layer, n_batch, six_d, nargs):
    if nargs == 2:
        lat = lambda b, t: (layer * MOD_ROWS + b, 0, 0)
        ctx = lambda b, t: (layer * MOD_ROWS + n_batch, 0, 0)
    elif nargs == 3:
        lat = lambda b, t, f: (layer * MOD_ROWS + b, 0, 0)
        ctx = lambda b, t, f: (layer * MOD_ROWS + n_batch, 0, 0)
    else:
        lat = lambda b, t, e, f: (layer * MOD_ROWS + b, 0, 0)
        ctx = lambda b, t, e, f: (layer * MOD_ROWS + n_batch, 0, 0)
    return [pl.BlockSpec((None, 1, six_d), lat), pl.BlockSpec((None, 1, six_d), ctx)]


def _even_in_kernel(x_ref, modl_ref, modc_ref, g_ref, w_ref, qg_ref, kg_ref, cos_ref, sa_ref, sb_ref,
                    hs_ref, ag_ref, q_ref, k_ref, v_ref, *, tt, n_lat):
    d = x_ref.shape[-1]
    conv2 = d
    qw = d // 2
    kvw = (w_ref.shape[1] - conv2 - qw) // 2
    is_ctx = _is_ctx_rows(pl.program_id(1), tt, n_lat)
    h = _norm_mod(x_ref[0], g_ref[...], is_ctx, modl_ref, modc_ref, 0).astype(BF16)
    half = conv2 // 2
    for c in range(2):
        ag_ref[0, :, c * half:(c + 1) * half] = jnp.dot(
            h, w_ref[:, c * half:(c + 1) * half], preferred_element_type=F32)
    cos, sa, sb = cos_ref[...], sa_ref[...], sb_ref[...]
    hs = hs_ref[...]

    def head_norm_rope(ys, gain):
        ss = jnp.dot((ys * ys).astype(BF16), hs, preferred_element_type=F32)
        yn = ys * lax.rsqrt(ss * (1.0 / HEAD_DIM) + NORM_EPS) * gain
        return _rope(yn, cos, sa, sb).astype(BF16)

    q = jnp.dot(h, w_ref[:, conv2:conv2 + qw], preferred_element_type=F32)
    for c in range(qw // LANES):
        q_ref[0, :, c * LANES:(c + 1) * LANES] = head_norm_rope(q[:, c * LANES:(c + 1) * LANES], qg_ref[...])
    kv = jnp.dot(h, w_ref[:, conv2 + qw:], preferred_element_type=F32)
    for c in range(kvw // LANES):
        k_ref[0, :, c * LANES:(c + 1) * LANES] = head_norm_rope(kv[:, c * LANES:(c + 1) * LANES], kg_ref[...])
    v_ref[0] = kv[:, kvw:].astype(BF16)


def _even_in(xs, mod3, layer, g1, w_in, qg, kg, tabs, hs, n_lat):
    n_batch, t_all, d = xs.shape
    tt = _pick_tile(t_all, 768)
    qw = d // 2
    kvw = (w_in.shape[1] - d - qw) // 2
    full2 = lambda b, t: (0, 0)
    tok = lambda b, t: (b, t, 0)
    tab_spec = pl.BlockSpec((tt, LANES), lambda b, t: (t, 0))
    return pl.pallas_call(
        functools.partial(_even_in_kernel, tt=tt, n_lat=n_lat),
        grid=(n_batch, t_all // tt),
        in_specs=[pl.BlockSpec((1, tt, d), tok)] + _mod_specs(layer, n_batch, 6 * d, 2) + [
            pl.BlockSpec((1, d), full2),
            pl.BlockSpec(w_in.shape, full2),
            pl.BlockSpec((1, LANES), full2), pl.BlockSpec((1, LANES), full2),
            tab_spec, tab_spec, tab_spec,
            pl.BlockSpec((LANES, LANES), full2)],
        out_specs=[pl.BlockSpec((1, tt, d), tok), pl.BlockSpec((1, tt, qw), tok),
                   pl.BlockSpec((1, tt, kvw), tok), pl.BlockSpec((1, tt, kvw), tok)],
        out_shape=[jax.ShapeDtypeStruct((n_batch, t_all, d), F32),
                   jax.ShapeDtypeStruct((n_batch, t_all, qw), BF16),
                   jax.ShapeDtypeStruct((n_batch, t_all, kvw), BF16),
                   jax.ShapeDtypeStruct((n_batch, t_all, kvw), BF16)],
        compiler_params=_params(("arbitrary", "arbitrary")),
        name="even_in",
    )(xs, mod3, mod3, g1, w_in, qg, kg, *tabs, hs)


def _pipe_item(n, n_items, lag, n_mid, nq):
    j = jnp.clip(n - lag, 0, n_items - 1)
    return j // (n_mid * nq), (j // nq) % n_mid, j % nq


def _gqa_replicate(src_ref, dst_ref, kv):
    a = src_ref[0].astype(F32)
    lane = lax.broadcasted_iota(jnp.int32, a.shape, 1)
    keep = jnp.logical_xor(lane < HEAD_DIM, kv == 1)
    rep = jnp.where(keep, a, pltpu.roll(a, HEAD_DIM, 1)).astype(BF16)
    for c in range(dst_ref.shape[-1] // LANES):
        dst_ref[:, c * LANES:(c + 1) * LANES] = rep


def _gqa_stack(q):
    head = lax.broadcasted_iota(jnp.int32, q.shape, 1) // HEAD_DIM
    group = q.shape[-1] // HEAD_DIM
    return jnp.concatenate([jnp.where(head == g, q, jnp.zeros_like(q)) for g in range(group)], axis=0)


def _gqa_unstack(o, rows):
    group = o.shape[-1] // HEAD_DIM
    head = lax.broadcasted_iota(jnp.int32, (rows, o.shape[-1]), 1) // HEAD_DIM
    out = jnp.zeros((rows, o.shape[-1]), F32)
    for g in range(group):
        out = out + jnp.where(head == g, o[g * rows:(g + 1) * rows], 0.0)
    return out.astype(BF16)


def _gqa_lat_kernel(q_ref, k_ref, v_ref, o_ref, k4_ref, v4_ref, s0, s1, p0, p1, l0, l1, *, n_items, nq, n_kv):
    n = pl.program_id(0)
    _, kv_qk, qi_qk = _pipe_item(n, n_items, 0, n_kv, nq)
    _, kv_pv, qi_pv = _pipe_item(n, n_items, 2, n_kv, nq)

    @pl.when(n == 0)
    def _():
        s1[...] = jnp.zeros_like(s1)
        p0[...] = jnp.zeros_like(p0)
        l0[...] = jnp.ones_like(l0)

    @pl.when((n < n_items) & (qi_qk == 0))
    def _():
        _gqa_replicate(k_ref, k4_ref, kv_qk)

    @pl.when((n == 0) | ((n >= 2) & (qi_pv == 0)))
    def _():
        _gqa_replicate(v_ref, v4_ref, kv_pv)

    def stages(s_w, s_r, p_w, l_w, p_r, l_r):
        s_w[...] = lax.dot_general(_gqa_stack(q_ref[0]), k4_ref[...], (((1,), (1,)), ((), ())),
                                   preferred_element_type=F32)
        s = s_r[...]
        p = jnp.exp2(s - jnp.max(s, axis=-1, keepdims=True))
        l_w[...] = jnp.sum(p, axis=-1, keepdims=True)
        p_w[...] = p.astype(BF16)
        o = jnp.dot(p_r[...], v4_ref[...], preferred_element_type=F32) / l_r[...]
        o_ref[0] = _gqa_unstack(o, o_ref.shape[1])

    @pl.when(n % 2 == 0)
    def _():
        stages(s0, s1, p1, l1, p0, l0)

    @pl.when(n % 2 == 1)
    def _():
        stages(s1, s0, p0, l0, p1, l1)


def _gqa_ctx_kernel(q_ref, k_ref, v_ref, o_ref, k4_ref, v4_ref):
    _gqa_replicate(k_ref, k4_ref, pl.program_id(1))
    _gqa_replicate(v_ref, v4_ref, pl.program_id(1))
    s = lax.dot_general(_gqa_stack(q_ref[0]), k4_ref[...], (((1,), (1,)), ((), ())),
                        preferred_element_type=F32)
    p = jnp.exp2(s - jnp.max(s, axis=-1, keepdims=True))
    l = jnp.sum(p, axis=-1, keepdims=True)
    o = jnp.dot(p.astype(BF16), v4_ref[...], preferred_element_type=F32) / l
    o_ref[0] = _gqa_unstack(o, o_ref.shape[1])


def _gqa(q, k, v, n_lat, n_ctx):
    n_batch, t_all, qw = q.shape
    kvw = k.shape[-1]
    n_kv = kvw // HEAD_DIM
    gw = qw // n_kv
    group = gw // HEAD_DIM
    nq = n_lat // TQ
    n_items = n_batch * n_kv * nq
    qk = lambda n: _pipe_item(n, n_items, 0, n_kv, nq)
    pv = lambda n: _pipe_item(n, n_items, 2, n_kv, nq)
    att = pl.pallas_call(
        functools.partial(_gqa_lat_kernel, n_items=n_items, nq=nq, n_kv=n_kv),
        grid=(n_items + 2,),
        in_specs=[pl.BlockSpec((1, TQ, gw), lambda n: (qk(n)[0], qk(n)[2], qk(n)[1])),
                  pl.BlockSpec((1, t_all, kvw), lambda n: (qk(n)[0], 0, 0)),
                  pl.BlockSpec((1, t_all, kvw), lambda n: (pv(n)[0], 0, 0))],
        out_specs=pl.BlockSpec((1, TQ, gw), lambda n: (pv(n)[0], pv(n)[2], pv(n)[1])),
        out_shape=jax.ShapeDtypeStruct((n_batch, n_lat, qw), BF16),
        scratch_shapes=[pltpu.VMEM((t_all, gw), BF16), pltpu.VMEM((t_all, gw), BF16),
                        pltpu.VMEM((group * TQ, t_all), F32), pltpu.VMEM((group * TQ, t_all), F32),
                        pltpu.VMEM((group * TQ, t_all), BF16), pltpu.VMEM((group * TQ, t_all), BF16),
                        pltpu.VMEM((group * TQ, 1), F32), pltpu.VMEM((group * TQ, 1), F32)],
        compiler_params=_params(("arbitrary",)),
        name="gqa_lat",
    )(q, k, v)
    cblk = n_lat // n_ctx
    att_ctx = pl.pallas_call(
        _gqa_ctx_kernel,
        grid=(n_batch, n_kv, n_ctx // TQ),
        in_specs=[pl.BlockSpec((1, TQ, gw), lambda b, h, i: (b, nq + i, h)),
                  pl.BlockSpec((1, n_ctx, kvw), lambda b, h, i: (b, cblk, 0)),
                  pl.BlockSpec((1, n_ctx, kvw), lambda b, h, i: (b, cblk, 0))],
        out_specs=pl.BlockSpec((1, TQ, gw), lambda b, h, i: (b, i, h)),
        out_shape=jax.ShapeDtypeStruct((n_batch, n_ctx, qw), BF16),
        scratch_shapes=[pltpu.VMEM((n_ctx, gw), BF16), pltpu.VMEM((n_ctx, gw), BF16)],
        compiler_params=_params(("arbitrary", "arbitrary", "arbitrary")),
        name="gqa_ctx",
    )(q, k, v)
    return jnp.concatenate([att, att_ctx], axis=1)


def _conv_kernel(ag_ref, w_ref, lng_ref, lnb_ref, o_ref, ypad_ref, *, n_lat, n_ctx):
    ch = o_ref.shape[-1]
    pad = CONV_WIDTH // 2
    halo = 16
    lat0 = halo
    ctx0 = 2 * halo + n_lat
    zeros = jnp.zeros((halo, ch), F32)
    ypad_ref[0:halo, :] = zeros
    ypad_ref[halo + n_lat:ctx0, :] = zeros
    ypad_ref[ctx0 + n_ctx:ctx0 + n_ctx + halo, :] = zeros

    def glu(src0, dst0, n):
        def body(i, carry):
            r = pl.multiple_of(i * CONV_CHUNK, CONV_CHUNK)
            a = ag_ref[0, pl.ds(src0 + r, CONV_CHUNK), 0:ch]
            g = ag_ref[0, pl.ds(src0 + r, CONV_CHUNK), ch:2 * ch]
            ypad_ref[pl.ds(dst0 + r, CONV_CHUNK), :] = a * jax.nn.sigmoid(g)
            return carry
        lax.fori_loop(0, n // CONV_CHUNK, body, 0)

    glu(0, lat0, n_lat)
    glu(n_lat, ctx0, n_ctx)

    def conv(src0, dst0, n):
        def body(i, carry):
            r = pl.multiple_of(i * CONV_CHUNK, CONV_CHUNK)
            accs = []
            for c in range(ch // LANES):
                cols = slice(c * LANES, (c + 1) * LANES)
                win = ypad_ref[pl.ds(src0 - halo + r, CONV_CHUNK + 2 * halo), cols]
                acc = jnp.zeros((CONV_CHUNK, LANES), F32)
                for s in range(8):
                    taps = [j for j in range(CONV_WIDTH) if (j + halo - pad) % 8 == s]
                    if not taps:
                        continue
                    sh = win[s:s + CONV_CHUNK + 2 * halo - 8]
                    for j in taps:
                        a8 = (j + halo - pad) - s
                        acc = acc + w_ref[j:j + 1, cols] * sh[a8:a8 + CONV_CHUNK]
                accs.append(acc)
            mu = sum(jnp.sum(a, axis=-1, keepdims=True) for a in accs) * (1.0 / ch)
            cens = [a - mu for a in accs]
            var = sum(jnp.sum(cn * cn, axis=-1, keepdims=True) for cn in cens) * (1.0 / ch)
            rstd = lax.rsqrt(var + LN_EPS)
            for c, cn in enumerate(cens):
                cols = slice(c * LANES, (c + 1) * LANES)
                yn = cn * rstd * lng_ref[:, cols] + lnb_ref[:, cols]
                o_ref[0, pl.ds(dst0 + r, CONV_CHUNK), cols] = (yn * jax.nn.sigmoid(yn)).astype(BF16)
            return carry
        lax.fori_loop(0, n // CONV_CHUNK, body, 0)

    conv(lat0, 0, n_lat)
    conv(ctx0, n_lat, n_ctx)


def _conv(ag, conv_w, ln_g, ln_b, n_lat, n_ctx):
    n_batch, t_all, two_ch = ag.shape
    ch = two_ch // 2
    full2 = lambda b: (0, 0)
    return pl.pallas_call(
        functools.partial(_conv_kernel, n_lat=n_lat, n_ctx=n_ctx),
        grid=(n_batch,),
        in_specs=[pl.BlockSpec((1, t_all, two_ch), lambda b: (b, 0, 0)),
                  pl.BlockSpec(conv_w.shape, full2),
                  pl.BlockSpec((1, ch), full2), pl.BlockSpec((1, ch), full2)],
        out_specs=pl.BlockSpec((1, t_all, ch), lambda b: (b, 0, 0)),
        out_shape=jax.ShapeDtypeStruct((n_batch, t_all, ch), BF16),
        scratch_shapes=[pltpu.VMEM((t_all + 48, ch), F32)],
        compiler_params=_params(("arbitrary",)),
        name="conv",
    )(ag, conv_w, ln_g, ln_b)


def _odd_in_kernel(x_ref, modl_ref, modc_ref, g_ref, w_ref, cos_ref, sa_ref, sb_ref,
                   q_ref, k_ref, v_ref, *, tt, n_lat):
    d = x_ref.shape[-1]
    is_ctx = _is_ctx_rows(pl.program_id(1), tt, n_lat)
    h = _norm_mod(x_ref[0], g_ref[...], is_ctx, modl_ref, modc_ref, 0).astype(BF16)
    cos, sa, sb = cos_ref[...], sa_ref[...], sb_ref[...]
    cw = 512
    for part, (dst, scale) in enumerate(((q_ref, ATT_SCALE), (k_ref, None), (v_ref, None))):
        for c0 in range(0, d, cw):
            y = jnp.dot(h, w_ref[:, part * d + c0:part * d + c0 + cw], preferred_element_type=F32)
            if part == 2:
                dst[0, :, c0:c0 + cw] = y.astype(BF16)
                continue
            for c in range(cw // LANES):
                ys = _rope(y[:, c * LANES:(c + 1) * LANES], cos, sa, sb)
                if scale is not None:
                    ys = ys * scale
                dst[0, :, c0 + c * LANES:c0 + (c + 1) * LANES] = ys.astype(BF16)


def _odd_in(xs, mod3, layer, g1, w_in, tabs, n_lat):
    n_batch, t_all, d = xs.shape
    tt = _pick_tile(t_all, 768)
    full2 = lambda b, t: (0, 0)
    tok = lambda b, t: (b, t, 0)
    tab_spec = pl.BlockSpec((tt, LANES), lambda b, t: (t, 0))
    return pl.pallas_call(
        functools.partial(_odd_in_kernel, tt=tt, n_lat=n_lat),
        grid=(n_batch, t_all // tt),
        in_specs=[pl.BlockSpec((1, tt, d), tok)] + _mod_specs(layer, n_batch, 6 * d, 2) + [
            pl.BlockSpec((1, d), full2),
            pl.BlockSpec(w_in.shape, full2),
            tab_spec, tab_spec, tab_spec],
        out_specs=[pl.BlockSpec((1, tt, d), tok)] * 3,
        out_shape=[jax.ShapeDtypeStruct((n_batch, t_all, d), BF16)] * 3,
        compiler_params=_params(("arbitrary", "arbitrary")),
        name="odd_in",
    )(xs, mod3, mod3, g1, w_in, *tabs)


def _diff_lambda(lam_ref, lam_init):
    lp = lam_ref[...]
    return (jnp.exp(jnp.sum(lp[0:1] * lp[1:2], axis=-1, keepdims=True))
            - jnp.exp(jnp.sum(lp[2:3] * lp[3:4], axis=-1, keepdims=True)) + lam_init)


def _diff_scores(q, k):
    first = lax.broadcasted_iota(jnp.int32, q.shape, 1) < HEAD_DIM
    zero = jnp.zeros_like(q)
    qs = jnp.concatenate([jnp.where(first, q, zero), jnp.where(first, zero, q)], axis=0)
    return lax.dot_general(qs, k, (((1,), (1,)), ((), ())), preferred_element_type=F32)


def _diff_weights(s, lam):
    rows = s.shape[0] // 2
    p = jnp.exp2(s - jnp.max(s, axis=-1, keepdims=True))
    p = p / jnp.sum(p, axis=-1, keepdims=True)
    return (p[0:rows] - lam * p[rows:]).astype(BF16)


def _diff_finish(o, sg_ref, lam_init):
    r = lax.rsqrt(jnp.mean(o * o, axis=-1, keepdims=True) + NORM_EPS)
    return ((o * r * sg_ref[...]) * (1.0 - lam_init)).astype(BF16)


def _diff_lat_kernel(q_ref, k_ref, v_ref, lam_ref, sg_ref, o_ref, va_ref, s0, s1, a0, a1,
                     *, lam_init, n_items, nq, n_heads):
    n = pl.program_id(0)
    lam = _diff_lambda(lam_ref, lam_init)
    _, _, qi_pv = _pipe_item(n, n_items, 2, n_heads, nq)

    @pl.when(n == 0)
    def _():
        s1[...] = jnp.zeros_like(s1)
        a0[...] = jnp.ones_like(a0)
        va_ref[:, v_ref.shape[-1]:] = jnp.ones((va_ref.shape[0], va_ref.shape[1] - v_ref.shape[-1]), BF16)

    @pl.when((n == 0) | ((n >= 2) & (qi_pv == 0)))
    def _():
        va_ref[:, 0:v_ref.shape[-1]] = v_ref[0]

    def stages(s_w, s_r, a_w, a_r):
        rows = q_ref.shape[1]
        vw = v_ref.shape[-1]
        s_w[...] = _diff_scores(q_ref[0], k_ref[0])
        s = s_r[...]
        a_w[...] = jnp.exp2(s - jnp.max(s, axis=-1, keepdims=True)).astype(BF16)
        oa = jnp.dot(a_r[...], va_ref[...], preferred_element_type=F32)
        o = oa[0:rows, 0:vw] / oa[0:rows, vw:vw + 1] - lam * (oa[rows:, 0:vw] / oa[rows:, vw:vw + 1])
        o_ref[0] = _diff_finish(o, sg_ref, lam_init)

    @pl.when(n % 2 == 0)
    def _():
        stages(s0, s1, a1, a0)

    @pl.when(n % 2 == 1)
    def _():
        stages(s1, s0, a0, a1)


def _diff_ctx_kernel(q_ref, k_ref, v_ref, lam_ref, sg_ref, o_ref, *, lam_init):
    a =_diff_weights(_diff_scores(q_ref[0], k_ref[0]), _diff_lambda(lam_ref, lam_init))
    o_ref[0] = _diff_finish(jnp.dot(a, v_ref[0], preferred_element_type=F32), sg_ref, lam_init)


def _diff_attn(q, k, v, lam_p, subln_g, lam_init, n_lat, n_ctx):
    n_batch, t_all, d = q.shape
    hw = 2 * HEAD_DIM
    n_heads = d // hw
    nq = n_lat // TQ
    n_items = n_batch * n_heads * nq
    qk = lambda n: _pipe_item(n, n_items, 0, n_heads, nq)
    pv = lambda n: _pipe_item(n, n_items, 2, n_heads, nq)
    att = pl.pallas_call(
        functools.partial(_diff_lat_kernel, lam_init=lam_init, n_items=n_items, nq=nq, n_heads=n_heads),
        grid=(n_items + 2,),
        in_specs=[pl.BlockSpec((1, TQ, hw), lambda n: (qk(n)[0], qk(n)[2], qk(n)[1])),
                  pl.BlockSpec((1, t_all, hw), lambda n: (qk(n)[0], 0, qk(n)[1])),
                  pl.BlockSpec((1, t_all, hw), lambda n: (pv(n)[0], 0, pv(n)[1])),
                  pl.BlockSpec(lam_p.shape, lambda n: (0, 0)),
                  pl.BlockSpec((1, hw), lambda n: (0, 0))],
        out_specs=pl.BlockSpec((1, TQ, hw), lambda n: (pv(n)[0], pv(n)[2], pv(n)[1])),
        out_shape=jax.ShapeDtypeStruct((n_batch, n_lat, d), BF16),
        scratch_shapes=[pltpu.VMEM((t_all, 2 * hw), BF16),
                        pltpu.VMEM((2 * TQ, t_all), F32), pltpu.VMEM((2 * TQ, t_all), F32),
                        pltpu.VMEM((2 * TQ, t_all), BF16), pltpu.VMEM((2 * TQ, t_all), BF16)],
        compiler_params=_params(("arbitrary",)),
        name="diff_lat",
    )(q, k, v, lam_p, subln_g)
    cblk = n_lat // n_ctx
    full2 = lambda b, h, i: (0, 0)
    att_ctx = pl.pallas_call(
        functools.partial(_diff_ctx_kernel, lam_init=lam_init),
        grid=(n_batch, n_heads, n_ctx // TQ),
        in_specs=[pl.BlockSpec((1, TQ, hw), lambda b, h, i: (b, nq + i, h)),
                  pl.BlockSpec((1, n_ctx, hw), lambda b, h, i: (b, cblk, h)),
                  pl.BlockSpec((1, n_ctx, hw), lambda b, h, i: (b, cblk, h)),
                  pl.BlockSpec(lam_p.shape, full2),
                  pl.BlockSpec((1, hw), full2)],
        out_specs=pl.BlockSpec((1, TQ, hw), lambda b, h, i: (b, i, h)),
        out_shape=jax.ShapeDtypeStruct((n_batch, n_ctx, d), BF16),
        compiler_params=_params(("arbitrary", "arbitrary", "arbitrary")),
        name="diff_ctx",
    )(q, k, v, lam_p, subln_g)
    return jnp.concatenate([att, att_ctx], axis=1)


def _out_kernel(*refs, tt, n_lat, n_parts):
    x_ref, modl_ref, modc_ref = refs[0:3]
    parts = refs[3:3 + n_parts]
    w_ref = refs[3 + n_parts]
    o_ref = refs[4 + n_parts]
    d = x_ref.shape[-1]
    is_ctx = _is_ctx_rows(pl.program_id(1), tt, n_lat)
    gate = _mod_pick(is_ctx, modl_ref, modc_ref, 2, d)
    y = jnp.zeros((tt, d), F32)
    row = 0
    for p in parts:
        kp = p.shape[-1]
        y = y + jnp.dot(p[0], w_ref[row:row + kp, :], preferred_element_type=F32)
        row += kp
    o_ref[0] = x_ref[0] + gate * y


def _out_proj(xs, mod3, layer, parts, w_out, n_lat):
    n_batch, t_all, d = xs.shape
    tt = _pick_tile(t_all, 768)
    tok = lambda b, t: (b, t, 0)
    return pl.pallas_call(
        functools.partial(_out_kernel, tt=tt, n_lat=n_lat, n_parts=len(parts)),
        grid=(n_batch, t_all // tt),
        in_specs=[pl.BlockSpec((1, tt, d), tok)] + _mod_specs(layer, n_batch, 6 * d, 2)
        + [pl.BlockSpec((1, tt, p.shape[-1]), tok) for p in parts]
        + [pl.BlockSpec(w_out.shape, lambda b, t: (0, 0))],
        out_specs=pl.BlockSpec((1, tt, d), tok),
        out_shape=jax.ShapeDtypeStruct(xs.shape, F32),
        compiler_params=_params(("arbitrary", "arbitrary")),
        name="out_proj",
    )(xs, mod3, mod3, *parts, w_out)


def _ffn_kernel(x_ref, modl_ref, modc_ref, g_ref, wg_ref, wu_ref, wd_ref, o_ref, h_ref, acc_ref, *, tt, n_lat):
    f = pl.program_id(2)
    d = x_ref.shape[-1]
    is_ctx = _is_ctx_rows(pl.program_id(1), tt, n_lat)

    @pl.when(f == 0)
    def _():
        h_ref[...] = _norm_mod(x_ref[0], g_ref[...], is_ctx, modl_ref, modc_ref, 3).astype(BF16)
        acc_ref[...] = jnp.zeros_like(acc_ref)

    h = h_ref[...]
    g = jnp.dot(h, wg_ref[...], preferred_element_type=F32)
    u = jnp.dot(h, wu_ref[...], preferred_element_type=F32)
    act = (g * jax.nn.sigmoid(g) * u).astype(BF16)
    acc_ref[...] += jnp.dot(act, wd_ref[...], preferred_element_type=F32)

    @pl.when(f == pl.num_programs(2) - 1)
    def _():
        gate = _mod_pick(is_ctx, modl_ref, modc_ref, 5, d)
        o_ref[0] = x_ref[0] + gate * acc_ref[...]


def _ffn(xs, mod3, layer, g2, wg, wu, wd, n_lat):
    n_batch, t_all, d = xs.shape
    ffn = wg.shape[-1]
    tt = _pick_tile(t_all, 768)
    tf = 512
    tok = lambda b, t, f: (b, t, 0)
    return pl.pallas_call(
        functools.partial(_ffn_kernel, tt=tt, n_lat=n_lat),
        grid=(n_batch, t_all // tt, ffn // tf),
        in_specs=[pl.BlockSpec((1, tt, d), tok)] + _mod_specs(layer, n_batch, 6 * d, 3) + [
            pl.BlockSpec((1, d), lambda b, t, f: (0, 0)),
            pl.BlockSpec((d, tf), lambda b, t, f: (0, f)),
            pl.BlockSpec((d, tf), lambda b, t, f: (0, f)),
            pl.BlockSpec((tf, d), lambda b, t, f: (f, 0))],
        out_specs=pl.BlockSpec((1, tt, d), tok),
        out_shape=jax.ShapeDtypeStruct(xs.shape, F32),
        scratch_shapes=[pltpu.VMEM((tt, d), BF16), pltpu.VMEM((tt, d), F32)],
        compiler_params=_params(("arbitrary", "arbitrary", "arbitrary")),
        name="ffn",
    )(xs, mod3, mod3, g2, wg, wu, wd)


def _top2(logits):
    lane = lax.broadcasted_iota(jnp.int32, logits.shape, 1).astype(F32)
    neg = jnp.float32(-jnp.inf)
    lg = jnp.where(lane < N_EXPERTS, logits, neg)
    m1 = jnp.max(lg, axis=-1, keepdims=True)
    i1 = jnp.min(jnp.where(lg == m1, lane, float(LANES)), axis=-1, keepdims=True)
    lg2 = jnp.where(lane == i1, neg, lg)
    m2 = jnp.max(lg2, axis=-1, keepdims=True)
    i2 = jnp.min(jnp.where(lg2 == m2, lane, float(LANES)), axis=-1, keepdims=True)
    e2 = jnp.exp(m2 - m1)
    den = 1.0 + e2
    return lane, i1, i2, 1.0 / den, e2 / den


def _route_kernel(x_ref, modl_ref, modc_ref, g_ref, rw_ref, info_ref, w_ref, cnt_ref, run_ref, *, tt, n_lat):
    @pl.when((pl.program_id(0) == 0) & (pl.program_id(1) == 0))
    def _():
        run_ref[...] = jnp.zeros_like(run_ref)

    is_ctx = _is_ctx_rows(pl.program_id(1), tt, n_lat)
    h = _norm_mod(x_ref[0], g_ref[...], is_ctx, modl_ref, modc_ref, 3)
    logits = jnp.dot(h, rw_ref[...], precision=lax.Precision.HIGHEST, preferred_element_type=F32)
    lane, i1, i2, w1, w2 = _top2(logits)
    chosen = jnp.where((lane == i1) | (lane == i2), 1.0, 0.0)
    before = (lax.broadcasted_iota(jnp.int32, (tt, tt), 1) < lax.broadcasted_iota(jnp.int32, (tt, tt), 0))
    pre = jnp.dot(jnp.where(before, 1.0, 0.0).astype(BF16), chosen.astype(BF16),
                  preferred_element_type=F32) + run_ref[...]
    r1 = jnp.sum(jnp.where(lane == i1, pre, 0.0), axis=-1, keepdims=True)
    r2 = jnp.sum(jnp.where(lane == i2, pre, 0.0), axis=-1, keepdims=True)
    run_ref[...] += jnp.sum(chosen, axis=0, keepdims=True)
    cnt_ref[...] = run_ref[...]
    info = jnp.where(lane == 0.0, i1, jnp.where(lane == 1.0, i2, jnp.where(lane == 2.0, r1,
                                                                          jnp.where(lane == 3.0, r2, 0.0))))
    info_ref[0] = info.T[0:8].astype(jnp.int32)
    w_ref[0] = jnp.where(lane == 0.0, w1, jnp.where(lane == 1.0, w2, 0.0))


def _route(xs, mod3, layer, g2, router_w, n_lat, nb, tt):
    n_batch, t_all, d = xs.shape
    rw = jnp.zeros((d, LANES), F32).at[:, :router_w.shape[1]].set(router_w)
    return pl.pallas_call(
        functools.partial(_route_kernel, tt=tt, n_lat=n_lat),
        grid=(n_batch, nb),
        in_specs=[pl.BlockSpec((1, tt, d), lambda b, j: (b, j, 0))] + _mod_specs(layer, n_batch, 6 * d, 2) + [
            pl.BlockSpec((1, d), lambda b, j: (0, 0)),
            pl.BlockSpec((d, LANES), lambda b, j: (0, 0))],
        out_specs=[pl.BlockSpec((1, 8, tt), lambda b, j: (b * nb + j, 0, 0)),
                   pl.BlockSpec((1, tt, LANES), lambda b, j: (b, j, 0)),
                   pl.BlockSpec((1, LANES), lambda b, j: (0, 0))],
        out_shape=[jax.ShapeDtypeStruct((n_batch * nb, 8, tt), jnp.int32),
                   jax.ShapeDtypeStruct((n_batch, nb * tt, LANES), F32),
                   jax.ShapeDtypeStruct((1, LANES), F32)],
        scratch_shapes=[pltpu.VMEM((1, LANES), F32)],
        compiler_params=_params(("arbitrary", "arbitrary")),
        name="moe_route",
    )(xs, mod3, mod3, g2, rw)


def _slot_row(starts_ref, info_ref, k, r, p):
    slot = starts_ref[info_ref[k, r]] + info_ref[2 + k, r]
    return pl.multiple_of(slot * p, p)


def _dispatch_kernel(starts_ref, fill_ref, nu_ref, x_ref, modl_ref, modc_ref, g_ref, info_ref, xs_ref,
                     hbuf_ref, zero_ref, sems, zsem, *, tt, n_lat, tm):
    nb = pl.num_programs(1)
    step = pl.program_id(0) * nb + pl.program_id(1)
    nsteps = pl.num_programs(0) * nb
    slot = lax.rem(step, 2)
    p = x_ref.shape[-1] // LANES

    def fill_copy(e):
        return pltpu.make_async_copy(zero_ref, xs_ref.at[pl.ds(pl.multiple_of(fill_ref[e] * p, p), tm * p)], zsem)

    @pl.when(step == 0)
    def _():
        zero_ref[...] = jnp.zeros_like(zero_ref)
        for e in range(N_EXPERTS):
            fill_copy(e).start()
        for e in range(N_EXPERTS):
            fill_copy(e).wait()

        def clear_tile(t, carry):
            cp = pltpu.make_async_copy(
                zero_ref, xs_ref.at[pl.ds(pl.multiple_of(t * (tm * p), tm * p), tm * p)], zsem)
            cp.start()
            cp.wait()
            return carry

        lax.fori_loop(nu_ref[0], xs_ref.shape[0] // (tm * p), clear_tile, 0)

    is_ctx = _is_ctx_rows(pl.program_id(1), tt, n_lat)
    h = _norm_mod(x_ref[0], g_ref[...], is_ctx, modl_ref, modc_ref, 3)
    for s in range(p):
        hbuf_ref[slot, pl.ds(s, tt, stride=p), :] = h[:, s * LANES:(s + 1) * LANES]

    def issue(r, carry):
        src = hbuf_ref.at[slot, pl.ds(pl.multiple_of(r * p, p), p)]
        for k in range(2):
            pltpu.make_async_copy(src, xs_ref.at[pl.ds(_slot_row(starts_ref, info_ref, k, r, p), p)],
                                  sems.at[slot]).start()
        return carry

    lax.fori_loop(0, tt, issue, 0, unroll=8)

    def wait_rows(sl):
        for _ in range(2):
            pltpu.make_async_copy(hbuf_ref.at[sl], xs_ref.at[pl.ds(0, tt * p)], sems.at[sl]).wait()

    @pl.when(step > 0)
    def _():
        wait_rows(1 - slot)

    @pl.when(step == nsteps - 1)
    def _():
        wait_rows(slot)


def _dispatch(xs, mod3, layer, g2, info, starts, fill, n_used, n_lat, nb, tt, tm, n_rows):
    n_batch, _, d = xs.shape
    p = d // LANES
    grid_spec = pltpu.PrefetchScalarGridSpec(
        num_scalar_prefetch=3,
        grid=(n_batch, nb),
        in_specs=[pl.BlockSpec((1, tt, d), lambda b, j, *_: (b, j, 0)),
                  pl.BlockSpec((None, 1, 6 * d), lambda b, j, *_: (layer * MOD_ROWS + b, 0, 0)),
                  pl.BlockSpec((None, 1, 6 * d), lambda b, j, *_: (layer * MOD_ROWS + n_batch, 0, 0)),
                  pl.BlockSpec((1, d), lambda b, j, *_: (0, 0)),
                  pl.BlockSpec((None, 8, tt), lambda b, j, *_: (b * nb + j, 0, 0), memory_space=pltpu.SMEM)],
        out_specs=pl.BlockSpec(memory_space=pl.ANY),
        scratch_shapes=[pltpu.VMEM((2, tt * p, LANES), F32), pltpu.VMEM((tm * p, LANES), F32),
                        pltpu.SemaphoreType.DMA((2,)), pltpu.SemaphoreType.DMA(())],
    )
    return pl.pallas_call(
        functools.partial(_dispatch_kernel, tt=tt, n_lat=n_lat, tm=tm),
        grid_spec=grid_spec,
        out_shape=jax.ShapeDtypeStruct((n_rows * p, LANES), F32),
        compiler_params=_params(("arbitrary", "arbitrary")),
        name="moe_dispatch",
    )(starts, fill, n_used, xs, mod3, mod3, g2, info)


def _experts_kernel(te_ref, nu_ref, xs_ref, wg_ref, wu_ref, wd_ref, o_ref, h_ref, acc_ref, *, tm):
    t = pl.program_id(0)
    f = pl.program_id(1)
    p = o_ref.shape[0] // tm
    live = t < nu_ref[0]

    @pl.when(live & (f == 0))
    def _():
        h_ref[...] = jnp.concatenate([xs_ref[pl.ds(s, tm, stride=p), :] for s in range(p)],
                                     axis=-1).astype(BF16)
        acc_ref[...] = jnp.zeros_like(acc_ref)

    @pl.when(live)
    def _():
        h = h_ref[...]
        g = jnp.dot(h, wg_ref[0], preferred_element_type=F32)
        u = jnp.dot(h, wu_ref[0], preferred_element_type=F32)
        act = (g * jax.nn.sigmoid(g) * u).astype(BF16)
        acc_ref[...] += jnp.dot(act, wd_ref[0], preferred_element_type=F32)

    @pl.when(live & (f == pl.num_programs(1) - 1))
    def _():
        for s in range(p):
            o_ref[pl.ds(s, tm, stride=p), :] = acc_ref[:, s * LANES:(s + 1) * LANES]

    @pl.when(jnp.logical_not(live) & (f == pl.num_programs(1) - 1))
    def _():
        o_ref[...] = jnp.zeros_like(o_ref)


def _experts(xsort, tile_expert, n_used, wg, wu, wd, tm, n_tiles):
    n_exp, d, ffn = wg.shape
    p = d // LANES
    tf = 512
    nf = ffn // tf

    def fcol(t, f, te, nu):
        return jnp.where(t < nu[0], f, nf - 1)

    grid_spec = pltpu.PrefetchScalarGridSpec(
        num_scalar_prefetch=2,
        grid=(n_tiles, nf),
        in_specs=[pl.BlockSpec((tm * p, LANES), lambda t, f, te, nu: (t, 0)),
                  pl.BlockSpec((1, d, tf), lambda t, f, te, nu: (te[t], 0, fcol(t, f, te, nu))),
                  pl.BlockSpec((1, d, tf), lambda t, f, te, nu: (te[t], 0, fcol(t, f, te, nu))),
                  pl.BlockSpec((1, tf, d), lambda t, f, te, nu: (te[t], fcol(t, f, te, nu), 0))],
        out_specs=pl.BlockSpec((tm * p, LANES), lambda t, f, te, nu: (t, 0)),
        scratch_shapes=[pltpu.VMEM((tm, d), BF16), pltpu.VMEM((tm, d), F32)],
    )
    return pl.pallas_call(
        functools.partial(_experts_kernel, tm=tm),
        grid_spec=grid_spec,
        out_shape=jax.ShapeDtypeStruct((n_tiles * tm * p, LANES), F32),
        compiler_params=_params(("arbitrary", "arbitrary")),
        name="moe_experts",
    )(tile_expert, n_used, xsort, wg, wu, wd)


def _combine_kernel(starts_ref, x_ref, modl_ref, modc_ref, w_ref, info0_ref, infon_ref, y_ref, fg_ref,
                    o_ref, buf1_ref, buf2_ref, sems, *, tt, n_lat, final):
    nb = pl.num_programs(1)
    step = pl.program_id(0) * nb + pl.program_id(1)
    nsteps = pl.num_programs(0) * nb
    slot = lax.rem(step, 2)
    d = x_ref.shape[-1]
    p = d // LANES
    bufs = (buf1_ref, buf2_ref)

    def issue_block(info_ref, sl):
        def issue(r, carry):
            for k in range(2):
                pltpu.make_async_copy(y_ref.at[pl.ds(_slot_row(starts_ref, info_ref, k, r, p), p)],
                                      bufs[k].at[sl, pl.ds(pl.multiple_of(r * p, p), p)], sems.at[sl]).start()
            return carry
        lax.fori_loop(0, tt, issue, 0, unroll=8)

    @pl.when(step == 0)
    def _():
        issue_block(info0_ref, 0)

    @pl.when(step + 1 < nsteps)
    def _():
        issue_block(infon_ref, 1 - slot)

    for k in range(2):
        pltpu.make_async_copy(y_ref.at[pl.ds(0, tt * p)], bufs[k].at[slot], sems.at[slot]).wait()

    is_ctx = _is_ctx_rows(pl.program_id(1), tt, n_lat)
    w = w_ref[0]
    w1, w2 = w[:, 0:1], w[:, 1:2]
    outs = []
    for s in range(p):
        cols = slice(s * LANES, (s + 1) * LANES)
        y = w1 * buf1_ref[slot, pl.ds(s, tt, stride=p), :] + w2 * buf2_ref[slot, pl.ds(s, tt, stride=p), :]
        gate = jnp.where(is_ctx, modc_ref[:, 5 * d + s * LANES:5 * d + (s + 1) * LANES],
                         modl_ref[:, 5 * d + s * LANES:5 * d + (s + 1) * LANES])
        xn = x_ref[0, :, cols] + gate * y
        if final:
            outs.append(xn)
        else:
            o_ref[0, :, cols] = xn
    if final:
        ss = sum(jnp.sum(xn * xn, axis=-1, keepdims=True) for xn in outs)
        r = lax.rsqrt(ss * (1.0 / d) + NORM_EPS)
        for s, xn in enumerate(outs):
            cols = slice(s * LANES, (s + 1) * LANES)
            o_ref[0, :, cols] = xn * r * fg_ref[:, cols]


def _combine(xs, mod3, layer, wts, info, starts, y, final_g, n_lat, nb, tt, final):
    n_batch, _, d = xs.shape
    p = d // LANES
    nblk = n_batch * nb
    out_rows = nb * tt
    grid_spec = pltpu.PrefetchScalarGridSpec(
        num_scalar_prefetch=1,
        grid=(n_batch, nb),
        in_specs=[pl.BlockSpec((1, tt, d), lambda b, j, *_: (b, j, 0)),
                  pl.BlockSpec((None, 1, 6 * d), lambda b, j, *_: (layer * MOD_ROWS + b, 0, 0)),
                  pl.BlockSpec((None, 1, 6 * d), lambda b, j, *_: (layer * MOD_ROWS + n_batch, 0, 0)),
                  pl.BlockSpec((1, tt, LANES), lambda b, j, *_: (b, j, 0)),
                  pl.BlockSpec((None, 8, tt), lambda b, j, *_: (0, 0, 0), memory_space=pltpu.SMEM),
                  pl.BlockSpec((None, 8, tt), lambda b, j, *_: (jnp.minimum(b * nb + j + 1, nblk - 1), 0, 0),
                               memory_space=pltpu.SMEM),
                  pl.BlockSpec(memory_space=pl.ANY),
                  pl.BlockSpec((1, d), lambda b, j, *_: (0, 0))],
        out_specs=pl.BlockSpec((1, tt, d), lambda b, j, *_: (b, j, 0)),
        scratch_shapes=[pltpu.VMEM((2, tt * p, LANES), F32), pltpu.VMEM((2, tt * p, LANES), F32),
                        pltpu.SemaphoreType.DMA((2,))],
    )
    return pl.pallas_call(
        functools.partial(_combine_kernel, tt=tt, n_lat=n_lat, final=final),
        grid_spec=grid_spec,
        out_shape=jax.ShapeDtypeStruct((n_batch, out_rows, d), F32),
        compiler_params=_params(("arbitrary", "arbitrary")),
        name="moe_combine",
    )(starts, xs, mod3, mod3, wts, info, info, y, final_g)


def _moe(xs, mod3, layer, g2, router_w, wg, wu, wd, final_g, n_lat, final):
    n_batch, t_all, d = xs.shape
    n_exp = wg.shape[0]
    tt = TQ
    tm = 512
    nb = (n_lat if final else t_all) // tt
    n_assign = 2 * n_batch * nb * tt
    n_tiles = -(-n_assign // tm) + n_exp
    info, wts, cnt = _route(xs, mod3, layer, g2, router_w, n_lat, nb, tt)
    counts = cnt[0, :n_exp].astype(jnp.int32)
    tiles_per = (counts + tm - 1) // tm
    tile_end = jnp.cumsum(tiles_per)
    starts = (tile_end - tiles_per) * tm
    n_used = tile_end[-1:]
    tile_ids = jnp.arange(n_tiles, dtype=jnp.int32)
    tile_expert = jnp.minimum(jnp.sum(tile_ids[:, None] >= tile_end[None, :], axis=1), n_exp - 1).astype(jnp.int32)
    xsort = _dispatch(xs, mod3, layer, g2, info, starts, starts + counts, n_used, n_lat, nb, tt, tm,
                      (n_tiles + 1) * tm)
    y = _experts(xsort, tile_expert, n_used, wg, wu, wd, tm, n_tiles)
    return _combine(xs, mod3, layer, wts, info, starts, y, final_g, n_lat, nb, tt, final)


def _final_kernel(x_ref, g_ref, o_ref):
    x = x_ref[0]
    o_ref[0] = x * lax.rsqrt(jnp.mean(x * x, axis=-1, keepdims=True) + NORM_EPS) * g_ref[...]


def _final_norm(xs, g, n_lat):
    n_batch, _, d = xs.shape
    tt = _pick_tile(n_lat, 256)
    return pl.pallas_call(
        _final_kernel,
        grid=(n_batch, n_lat // tt),
        in_specs=[pl.BlockSpec((1, tt, d), lambda b, t: (b, t, 0)), pl.BlockSpec((1, d), lambda b, t: (0, 0))],
        out_specs=pl.BlockSpec((1, tt, d), lambda b, t: (b, t, 0)),
        out_shape=jax.ShapeDtypeStruct((n_batch, n_lat, d), F32),
        compiler_params=_params(("arbitrary", "arbitrary")),
        name="final_norm",
    )(xs, g)


def kernel(x, c, ctx, c_ctx, ada_w, ada_b, norm1_g, norm2_g, ev_w_in, ev_conv_w, ev_ln_g, ev_ln_b,
           ev_q_norm_g, ev_k_norm_g, ev_w_out, ev_ffn_wg, ev_ffn_wu, ev_ffn_wd, od_w_in, od_lam,
           od_subln_g, od_w_out, od_router_w, od_moe_wg, od_moe_wu, od_moe_wd, final_norm_g):
    n_batch, n_lat, d = x.shape
    n_ctx = ctx.shape[1]
    depth = ada_w.shape[0]
    assert n_batch + 1 <= MOD_ROWS and n_lat % n_ctx == 0 and n_ctx % TQ == 0 and d % LANES == 0

    cond = jnp.zeros((MOD_ROWS, d), F32).at[:n_batch].set(c).at[n_batch].set(c_ctx)
    mod3 = _adaln(cond, ada_w, ada_b).reshape(depth * MOD_ROWS, 1, 6 * d)
    tabs = _rope_tables(n_lat, n_ctx)
    hs = _head_sum_matrix()
    xs = jnp.concatenate([x, ctx], axis=1)

    for layer in range(depth):
        i = layer // 2
        g1 = norm1_g[layer].reshape(1, d)
        g2 = norm2_g[layer].reshape(1, d)
        if layer % 2 == 0:
            rep = LANES // HEAD_DIM
            qg = (jnp.tile(ev_q_norm_g[i], rep) * ATT_SCALE).reshape(1, LANES)
            kg = jnp.tile(ev_k_norm_g[i], rep).reshape(1, LANES)
            ag, q, k, v = _even_in(xs, mod3, layer, g1, ev_w_in[i].astype(BF16), qg, kg, tabs, hs, n_lat)
            att = _gqa(q, k, v, n_lat, n_ctx)
            cv = _conv(ag, ev_conv_w[i], ev_ln_g[i].reshape(1, -1), ev_ln_b[i].reshape(1, -1), n_lat, n_ctx)
            xs = _out_proj(xs, mod3, layer, [cv, att], ev_w_out[i].astype(BF16), n_lat)
            xs = _ffn(xs, mod3, layer, g2, ev_ffn_wg[i].astype(BF16), ev_ffn_wu[i].astype(BF16),
                      ev_ffn_wd[i].astype(BF16), n_lat)
        else:
            lam_init = 0.8 - 0.6 * math.exp(-0.3 * layer)
            q, k, v = _odd_in(xs, mod3, layer, g1, od_w_in[i].astype(BF16), tabs, n_lat)
            att = _diff_attn(q, k, v, od_lam[i], od_subln_g[i].reshape(1, -1), lam_init, n_lat, n_ctx)
            xs = _out_proj(xs, mod3, layer, [att], od_w_out[i].astype(BF16), n_lat)
            xs = _moe(xs, mod3, layer, g2, od_router_w[i], od_moe_wg[i].astype(BF16),
                      od_moe_wu[i].astype(BF16), od_moe_wd[i].astype(BF16), final_norm_g.reshape(1, d),
                      n_lat, final=layer == depth - 1)
    if depth % 2 == 0:
        return xs
    return _final_norm(xs, final_norm_g.reshape(1, d), n_lat)
```

```python
import functools
import math

import jax
import jax.numpy as jnp
from jax import lax
from jax.experimental import pallas as pl
from jax.experimental.pallas import tpu as pltpu

F32 = jnp.float32
BF16 = jnp.bfloat16

HEAD_DIM = 64
GRID_W = 64
ROPE_BASE = 10000.0
NORM_EPS = 1e-6
LN_EPS = 1e-5
CONV_WIDTH = 31
N_EXPERTS = 8
LANES = 128
MOD_ROWS = 24
VMEM_LIMIT = 56 * 1024 * 1024
ATT_SCALE = HEAD_DIM ** -0.5 * math.log2(math.e)
TQ = 256
FFN_TILE = 512
CONV_CHUNK = 64


def _params(sem):
    return pltpu.CompilerParams(dimension_semantics=sem, vmem_limit_bytes=VMEM_LIMIT)


def _pick_tile(total, target):
    best = 8
    for cand in range(8, min(total, target) + 1, 8):
        if total % cand == 0:
            best = cand
    return best


def _adaln_kernel(s_ref, w_ref, b_ref, o_ref):
    s = s_ref[...]
    s = s * jax.nn.sigmoid(s)
    o_ref[0] = jnp.dot(s, w_ref[0], precision=lax.Precision.HIGHEST,
                       preferred_element_type=F32) + b_ref[0]


def _adaln(cond, ada_w, ada_b):
    depth, d, six_d = ada_w.shape
    tn = six_d // 4
    return pl.pallas_call(
        _adaln_kernel,
        grid=(depth, six_d // tn),
        in_specs=[pl.BlockSpec((MOD_ROWS, d), lambda l, n: (0, 0)),
                  pl.BlockSpec((1, d, tn), lambda l, n: (l, 0, n)),
                  pl.BlockSpec((1, 1, tn), lambda l, n: (l, 0, n))],
        out_specs=pl.BlockSpec((1, MOD_ROWS, tn), lambda l, n: (l, 0, n)),
        out_shape=jax.ShapeDtypeStruct((depth, MOD_ROWS, six_d), F32),
        compiler_params=_params(("arbitrary", "arbitrary")),
        name="adaln",
    )(cond, ada_w, ada_b.reshape(depth, 1, six_d))


def _is_ctx_rows(t, tt, n_lat):
    rows = t * tt + lax.broadcasted_iota(jnp.int32, (tt, 1), 0)
    return rows >= n_lat


def _mod_pick(is_ctx, modl_ref, modc_ref, idx, d):
    return jnp.where(is_ctx, modc_ref[:, idx * d:(idx + 1) * d], modl_ref[:, idx * d:(idx + 1) * d])


def _norm_mod(x, g, is_ctx, modl_ref, modc_ref, shift_idx):
    d = x.shape[-1]
    r = lax.rsqrt(jnp.mean(x * x, axis=-1, keepdims=True) + NORM_EPS)
    shift = _mod_pick(is_ctx, modl_ref, modc_ref, shift_idx, d)
    scale = _mod_pick(is_ctx, modl_ref, modc_ref, shift_idx + 1, d)
    return (x * r * g) * (1.0 + scale) + shift


def _rope(xs, cos, sin_a, sin_b):
    return xs * cos + pltpu.roll(xs, LANES - 16, 1) * sin_a + pltpu.roll(xs, 16, 1) * sin_b


def _rope_tables(n_lat, n_ctx):
    rows = n_lat // GRID_W
    r = jnp.broadcast_to(jnp.arange(rows, dtype=F32)[:, None], (rows, GRID_W)).reshape(-1)
    col = jnp.broadcast_to(jnp.arange(GRID_W, dtype=F32)[None, :], (rows, GRID_W)).reshape(-1)
    quarter = HEAD_DIM // 4
    inv_freq = ROPE_BASE ** (-jnp.arange(quarter, dtype=F32) / quarter)
    ang_r = r[:, None] * inv_freq
    ang_c = col[:, None] * inv_freq
    zeros = jnp.zeros_like(ang_r)
    cos64 = jnp.concatenate([jnp.cos(ang_r), jnp.cos(ang_r), jnp.cos(ang_c), jnp.cos(ang_c)], axis=1)
    sin_a64 = jnp.concatenate([-jnp.sin(ang_r), zeros, -jnp.sin(ang_c), zeros], axis=1)
    sin_b64 = jnp.concatenate([zeros, jnp.sin(ang_r), zeros, jnp.sin(ang_c)], axis=1)

    def full(tab, ctx_val):
        tab = jnp.concatenate([tab, jnp.full((n_ctx, HEAD_DIM), ctx_val, F32)], axis=0)
        return jnp.tile(tab, (1, LANES // HEAD_DIM))

    return full(cos64, 1.0), full(sin_a64, 0.0), full(sin_b64, 0.0)


def _head_sum_matrix():
    i = jnp.arange(LANES) // HEAD_DIM
    return (i[:, None] == i[None, :]).astype(BF16)


def _mod_specs(layer, n_batch, six_d, nargs):
    if nargs == 2:
        lat = lambda b, t: (layer * MOD_ROWS + b, 0, 0)
        ctx = lambda b, t: (layer * MOD_ROWS + n_batch, 0, 0)
    elif nargs == 3:
        lat = lambda b, t, f: (layer * MOD_ROWS + b, 0, 0)
        ctx = lambda b, t, f: (layer * MOD_ROWS + n_batch, 0, 0)
    else:
        lat = lambda b, t, e, f: (layer * MOD_ROWS + b, 0, 0)
        ctx = lambda b, t, e, f: (layer * MOD_ROWS + n_batch, 0, 0)
    return [pl.BlockSpec((None, 1, six_d), lat), pl.BlockSpec((None, 1, six_d), ctx)]


def _even_in_kernel(x_ref, modl_ref, modc_ref, g_ref, w_ref, qg_ref, kg_ref, cos_ref, sa_ref, sb_ref,
                    hs_ref, ag_ref, q_ref, k_ref, v_ref, *, tt, n_lat):
    d = x_ref.shape[-1]
    conv2 = d
    qw = d // 2
    kvw = (w_ref.shape[1] - conv2 - qw) // 2
    is_ctx = _is_ctx_rows(pl.program_id(1), tt, n_lat)
    h = _norm_mod(x_ref[0], g_ref[...], is_ctx, modl_ref, modc_ref, 0).astype(BF16)
    half = conv2 // 2
    for c in range(2):
        ag_ref[0, :, c * half:(c + 1) * half] = jnp.dot(
            h, w_ref[:, c * half:(c + 1) * half], preferred_element_type=F32)
    cos, sa, sb = cos_ref[...], sa_ref[...], sb_ref[...]
    hs = hs_ref[...]

    def head_norm_rope(ys, gain):
        ss = jnp.dot((ys * ys).astype(BF16), hs, preferred_element_type=F32)
        yn = ys * lax.rsqrt(ss * (1.0 / HEAD_DIM) + NORM_EPS) * gain
        return _rope(yn, cos, sa, sb).astype(BF16)

    q = jnp.dot(h, w_ref[:, conv2:conv2 + qw], preferred_element_type=F32)
    for c in range(qw // LANES):
        q_ref[0, :, c * LANES:(c + 1) * LANES] = head_norm_rope(q[:, c * LANES:(c + 1) * LANES], qg_ref[...])
    kv = jnp.dot(h, w_ref[:, conv2 + qw:], preferred_element_type=F32)
    for c in range(kvw // LANES):
        k_ref[0, :, c * LANES:(c + 1) * LANES] = head_norm_rope(kv[:, c * LANES:(c + 1) * LANES], kg_ref[...])
    v_ref[0] = kv[:, kvw:].astype(BF16)


def _even_in(xs, mod3, layer, g1, w_in, qg, kg, tabs, hs, n_lat):
    n_batch, t_all, d = xs.shape
    tt = _pick_tile(t_all, 768)
    qw = d // 2
    kvw = (w_in.shape[1] - d - qw) // 2
    full2 = lambda b, t: (0, 0)
    tok = lambda b, t: (b, t, 0)
    tab_spec = pl.BlockSpec((tt, LANES), lambda b, t: (t, 0))
    return pl.pallas_call(
        functools.partial(_even_in_kernel, tt=tt, n_lat=n_lat),
        grid=(n_batch, t_all // tt),
        in_specs=[pl.BlockSpec((1, tt, d), tok)] + _mod_specs(layer, n_batch, 6 * d, 2) + [
            pl.BlockSpec((1, d), full2),
            pl.BlockSpec(w_in.shape, full2),
            pl.BlockSpec((1, LANES), full2), pl.BlockSpec((1, LANES), full2),
            tab_spec, tab_spec, tab_spec,
            pl.BlockSpec((LANES, LANES), full2)],
        out_specs=[pl.BlockSpec((1, tt, d), tok), pl.BlockSpec((1, tt, qw), tok),
                   pl.BlockSpec((1, tt, kvw), tok), pl.BlockSpec((1, tt, kvw), tok)],
        out_shape=[jax.ShapeDtypeStruct((n_batch, t_all, d), F32),
                   jax.ShapeDtypeStruct((n_batch, t_all, qw), BF16),
                   jax.ShapeDtypeStruct((n_batch, t_all, kvw), BF16),
                   jax.ShapeDtypeStruct((n_batch, t_all, kvw), BF16)],
        compiler_params=_params(("arbitrary", "arbitrary")),
        name="even_in",
    )(xs, mod3, mod3, g1, w_in, qg, kg, *tabs, hs)


def _pipe_item(n, n_items, lag, n_mid, nq):
    j = jnp.clip(n - lag, 0, n_items - 1)
    return j // (n_mid * nq), (j // nq) % n_mid, j % nq


def _gqa_replicate(src_ref, dst_ref, kv):
    a = src_ref[0].astype(F32)
    lane = lax.broadcasted_iota(jnp.int32, a.shape, 1)
    keep = jnp.logical_xor(lane < HEAD_DIM, kv == 1)
    rep = jnp.where(keep, a, pltpu.roll(a, HEAD_DIM, 1)).astype(BF16)
    for c in range(dst_ref.shape[-1] // LANES):
        dst_ref[:, c * LANES:(c + 1) * LANES] = rep


def _gqa_stack(q):
    head = lax.broadcasted_iota(jnp.int32, q.shape, 1) // HEAD_DIM
    group = q.shape[-1] // HEAD_DIM
    return jnp.concatenate([jnp.where(head == g, q, jnp.zeros_like(q)) for g in range(group)], axis=0)


def _gqa_unstack(o, rows):
    group = o.shape[-1] // HEAD_DIM
    head = lax.broadcasted_iota(jnp.int32, (rows, o.shape[-1]), 1) // HEAD_DIM
    out = jnp.zeros((rows, o.shape[-1]), F32)
    for g in range(group):
        out = out + jnp.where(head == g, o[g * rows:(g + 1) * rows], 0.0)
    return out.astype(BF16)


def _gqa_lat_kernel(q_ref, k_ref, v_ref, o_ref, k4_ref, v4_ref, s0, s1, p0, p1, l0, l1, *, n_items, nq, n_kv):
    n = pl.program_id(0)
    _, kv_qk, qi_qk = _pipe_item(n, n_items, 0, n_kv, nq)
    _, kv_pv, qi_pv = _pipe_item(n, n_items, 2, n_kv, nq)

    @pl.when(n == 0)
    def _():
        s1[...] = jnp.zeros_like(s1)
        p0[...] = jnp.zeros_like(p0)
        l0[...] = jnp.ones_like(l0)

    @pl.when((n < n_items) & (qi_qk == 0))
    def _():
        _gqa_replicate(k_ref, k4_ref, kv_qk)

    @pl.when((n == 0) | ((n >= 2) & (qi_pv == 0)))
    def _():
        _gqa_replicate(v_ref, v4_ref, kv_pv)

    def stages(s_w, s_r, p_w, l_w, p_r, l_r):
        s_w[...] = lax.dot_general(_gqa_stack(q_ref[0]), k4_ref[...], (((1,), (1,)), ((), ())),
                                   preferred_element_type=F32)
        s = s_r[...]
        p = jnp.exp2(s - jnp.max(s, axis=-1, keepdims=True))
        l_w[...] = jnp.sum(p, axis=-1, keepdims=True)
        p_w[...] = p.astype(BF16)
        o = jnp.dot(p_r[...], v4_ref[...], preferred_element_type=F32) / l_r[...]
        o_ref[0] = _gqa_unstack(o, o_ref.shape[1])

    @pl.when(n % 2 == 0)
    def _():
        stages(s0, s1, p1, l1, p0, l0)

    @pl.when(n % 2 == 1)
    def _():
        stages(s1, s0, p0, l0, p1, l1)


def _gqa_ctx_kernel(q_ref, k_ref, v_ref, o_ref, k4_ref, v4_ref):
    _gqa_replicate(k_ref, k4_ref, pl.program_id(1))
    _gqa_replicate(v_ref, v4_ref, pl.program_id(1))
    s = lax.dot_general(_gqa_stack(q_ref[0]), k4_ref[...], (((1,), (1,)), ((), ())),
                        preferred_element_type=F32)
    p = jnp.exp2(s - jnp.max(s, axis=-1, keepdims=True))
    l = jnp.sum(p, axis=-1, keepdims=True)
    o = jnp.dot(p.astype(BF16), v4_ref[...], preferred_element_type=F32) / l
    o_ref[0] = _gqa_unstack(o, o_ref.shape[1])


def _gqa(q, k, v, n_lat, n_ctx):
    n_batch, t_all, qw = q.shape
    kvw = k.shape[-1]
    n_kv = kvw // HEAD_DIM
    gw = qw // n_kv
    group = gw // HEAD_DIM
    nq = n_lat // TQ
    n_items = n_batch * n_kv * nq
    qk = lambda n: _pipe_item(n, n_items, 0, n_kv, nq)
    pv = lambda n: _pipe_item(n, n_items, 2, n_kv, nq)
    att = pl.pallas_call(
        functools.partial(_gqa_lat_kernel, n_items=n_items, nq=nq, n_kv=n_kv),
        grid=(n_items + 2,),
        in_specs=[pl.BlockSpec((1, TQ, gw), lambda n: (qk(n)[0], qk(n)[2], qk(n)[1])),
                  pl.BlockSpec((1, t_all, kvw), lambda n: (qk(n)[0], 0, 0)),
                  pl.BlockSpec((1, t_all, kvw), lambda n: (pv(n)[0], 0, 0))],
        out_specs=pl.BlockSpec((1, TQ, gw), lambda n: (pv(n)[0], pv(n)[2], pv(n)[1])),
        out_shape=jax.ShapeDtypeStruct((n_batch, n_lat, qw), BF16),
        scratch_shapes=[pltpu.VMEM((t_all, gw), BF16), pltpu.VMEM((t_all, gw), BF16),
                        pltpu.VMEM((group * TQ, t_all), F32), pltpu.VMEM((group * TQ, t_all), F32),
                        pltpu.VMEM((group * TQ, t_all), BF16), pltpu.VMEM((group * TQ, t_all), BF16),
                        pltpu.VMEM((group * TQ, 1), F32), pltpu.VMEM((group * TQ, 1), F32)],
        compiler_params=_params(("arbitrary",)),
        name="gqa_lat",
    )(q, k, v)
    cblk = n_lat // n_ctx
    att_ctx = pl.pallas_call(
        _gqa_ctx_kernel,
        grid=(n_batch, n_kv, n_ctx // TQ),
        in_specs=[pl.BlockSpec((1, TQ, gw), lambda b, h, i: (b, nq + i, h)),
                  pl.BlockSpec((1, n_ctx, kvw), lambda b, h, i: (b, cblk, 0)),
                  pl.BlockSpec((1, n_ctx, kvw), lambda b, h, i: (b, cblk, 0))],
        out_specs=pl.BlockSpec((1, TQ, gw), lambda b, h, i: (b, i, h)),
        out_shape=jax.ShapeDtypeStruct((n_batch, n_ctx, qw), BF16),
        scratch_shapes=[pltpu.VMEM((n_ctx, gw), BF16), pltpu.VMEM((n_ctx, gw), BF16)],
        compiler_params=_params(("arbitrary", "arbitrary", "arbitrary")),
        name="gqa_ctx",
    )(q, k, v)
    return jnp.concatenate([att, att_ctx], axis=1)


def _conv_kernel(ag_ref, w_ref, lng_ref, lnb_ref, o_ref, ypad_ref, *, n_lat, n_ctx):
    ch = o_ref.shape[-1]
    pad = CONV_WIDTH // 2
    halo = 16
    lat0 = halo
    ctx0 = 2 * halo + n_lat
    zeros = jnp.zeros((halo, ch), F32)
    ypad_ref[0:halo, :] = zeros
    ypad_ref[halo + n_lat:ctx0, :] = zeros
    ypad_ref[ctx0 + n_ctx:ctx0 + n_ctx + halo, :] = zeros

    def glu(src0, dst0, n):
        def body(i, carry):
            r = pl.multiple_of(i * CONV_CHUNK, CONV_CHUNK)
            a = ag_ref[0, pl.ds(src0 + r, CONV_CHUNK), 0:ch]
            g = ag_ref[0, pl.ds(src0 + r, CONV_CHUNK), ch:2 * ch]
            ypad_ref[pl.ds(dst0 + r, CONV_CHUNK), :] = a * jax.nn.sigmoid(g)
            return carry
        lax.fori_loop(0, n // CONV_CHUNK, body, 0)

    glu(0, lat0, n_lat)
    glu(n_lat, ctx0, n_ctx)

    def conv(src0, dst0, n):
        def body(i, carry):
            r = pl.multiple_of(i * CONV_CHUNK, CONV_CHUNK)
            accs = []
            for c in range(ch // LANES):
                cols = slice(c * LANES, (c + 1) * LANES)
                win = ypad_ref[pl.ds(src0 - halo + r, CONV_CHUNK + 2 * halo), cols]
                acc = jnp.zeros((CONV_CHUNK, LANES), F32)
                for s in range(8):
                    taps = [j for j in range(CONV_WIDTH) if (j + halo - pad) % 8 == s]
                    if not taps:
                        continue
                    sh = win[s:s + CONV_CHUNK + 2 * halo - 8]
                    for j in taps:
                        a8 = (j + halo - pad) - s
                        acc = acc + w_ref[j:j + 1, cols] * sh[a8:a8 + CONV_CHUNK]
                accs.append(acc)
            mu = sum(jnp.sum(a, axis=-1, keepdims=True) for a in accs) * (1.0 / ch)
            cens = [a - mu for a in accs]
            var = sum(jnp.sum(cn * cn, axis=-1, keepdims=True) for cn in cens) * (1.0 / ch)
            rstd = lax.rsqrt(var + LN_EPS)
            for c, cn in enumerate(cens):
                cols = slice(c * LANES, (c + 1) * LANES)
                yn = cn * rstd * lng_ref[:, cols] + lnb_ref[:, cols]
                o_ref[0, pl.ds(dst0 + r, CONV_CHUNK), cols] = (yn * jax.nn.sigmoid(yn)).astype(BF16)
            return carry
        lax.fori_loop(0, n // CONV_CHUNK, body, 0)

    conv(lat0, 0, n_lat)
    conv(ctx0, n_lat, n_ctx)


def _conv(ag, conv_w, ln_g, ln_b, n_lat, n_ctx):
    n_batch, t_all, two_ch = ag.shape
    ch = two_ch // 2
    full2 = lambda b: (0, 0)
    return pl.pallas_call(
        functools.partial(_conv_kernel, n_lat=n_lat, n_ctx=n_ctx),
        grid=(n_batch,),
        in_specs=[pl.BlockSpec((1, t_all, two_ch), lambda b: (b, 0, 0)),
                  pl.BlockSpec(conv_w.shape, full2),
                  pl.BlockSpec((1, ch), full2), pl.BlockSpec((1, ch), full2)],
        out_specs=pl.BlockSpec((1, t_all, ch), lambda b: (b, 0, 0)),
        out_shape=jax.ShapeDtypeStruct((n_batch, t_all, ch), BF16),
        scratch_shapes=[pltpu.VMEM((t_all + 48, ch), F32)],
        compiler_params=_params(("arbitrary",)),
        name="conv",
    )(ag, conv_w, ln_g, ln_b)


def _odd_in_kernel(x_ref, modl_ref, modc_ref, g_ref, w_ref, cos_ref, sa_ref, sb_ref,
                   q_ref, k_ref, v_ref, *, tt, n_lat):
    d = x_ref.shape[-1]
    is_ctx = _is_ctx_rows(pl.program_id(1), tt, n_lat)
    h = _norm_mod(x_ref[0], g_ref[...], is_ctx, modl_ref, modc_ref, 0).astype(BF16)
    cos, sa, sb = cos_ref[...], sa_ref[...], sb_ref[...]
    cw = 512
    for part, (dst, scale) in enumerate(((q_ref, ATT_SCALE), (k_ref, None), (v_ref, None))):
        for c0 in range(0, d, cw):
            y = jnp.dot(h, w_ref[:, part * d + c0:part * d + c0 + cw], preferred_element_type=F32)
            if part == 2:
                dst[0, :, c0:c0 + cw] = y.astype(BF16)
                continue
            for c in range(cw // LANES):
                ys = _rope(y[:, c * LANES:(c + 1) * LANES], cos, sa, sb)
                if scale is not None:
                    ys = ys * scale
                dst[0, :, c0 + c * LANES:c0 + (c + 1) * LANES] = ys.astype(BF16)


def _odd_in(xs, mod3, layer, g1, w_in, tabs, n_lat):
    n_batch, t_all, d = xs.shape
    tt = _pick_tile(t_all, 768)
    full2 = lambda b, t: (0, 0)
    tok = lambda b, t: (b, t, 0)
    tab_spec = pl.BlockSpec((tt, LANES), lambda b, t: (t, 0))
    return pl.pallas_call(
        functools.partial(_odd_in_kernel, tt=tt, n_lat=n_lat),
        grid=(n_batch, t_all // tt),
        in_specs=[pl.BlockSpec((1, tt, d), tok)] + _mod_specs(layer, n_batch, 6 * d, 2) + [
            pl.BlockSpec((1, d), full2),
            pl.BlockSpec(w_in.shape, full2),
            tab_spec, tab_spec, tab_spec],
        out_specs=[pl.BlockSpec((1, tt, d), tok)] * 3,
        out_shape=[jax.ShapeDtypeStruct((n_batch, t_all, d), BF16)] * 3,
        compiler_params=_params(("arbitrary", "arbitrary")),
        name="odd_in",
    )(xs, mod3, mod3, g1, w_in, *tabs)


def _diff_lambda(lam_ref, lam_init):
    lp = lam_ref[...]
    return (jnp.exp(jnp.sum(lp[0:1] * lp[1:2], axis=-1, keepdims=True))
            - jnp.exp(jnp.sum(lp[2:3] * lp[3:4], axis=-1, keepdims=True)) + lam_init)


def _diff_scores(q, k):
    first = lax.broadcasted_iota(jnp.int32, q.shape, 1) < HEAD_DIM
    zero = jnp.zeros_like(q)
    qs = jnp.concatenate([jnp.where(first, q, zero), jnp.where(first, zero, q)], axis=0)
    return lax.dot_general(qs, k, (((1,), (1,)), ((), ())), preferred_element_type=F32)


def _diff_weights(s, lam):
    rows = s.shape[0] // 2
    p = jnp.exp2(s - jnp.max(s, axis=-1, keepdims=True))
    p = p / jnp.sum(p, axis=-1, keepdims=True)
    return (p[0:rows] - lam * p[rows:]).astype(BF16)


def _diff_finish(o, sg_ref, lam_init):
    r = lax.rsqrt(jnp.mean(o * o, axis=-1, keepdims=True) + NORM_EPS)
    return ((o * r * sg_ref[...]) * (1.0 - lam_init)).astype(BF16)


def _diff_lat_kernel(q_ref, k_ref, v_ref, lam_ref, sg_ref, o_ref, va_ref, s0, s1, a0, a1,
                     *, lam_init, n_items, nq, n_heads):
    n = pl.program_id(0)
    lam = _diff_lambda(lam_ref, lam_init)
    _, _, qi_pv = _pipe_item(n, n_items, 2, n_heads, nq)

    @pl.when(n == 0)
    def _():
        s1[...] = jnp.zeros_like(s1)
        a0[...] = jnp.ones_like(a0)
        va_ref[:, v_ref.shape[-1]:] = jnp.ones((va_ref.shape[0], va_ref.shape[1] - v_ref.shape[-1]), BF16)

    @pl.when((n == 0) | ((n >= 2) & (qi_pv == 0)))
    def _():
        va_ref[:, 0:v_ref.shape[-1]] = v_ref[0]

    def stages(s_w, s_r, a_w, a_r):
        rows = q_ref.shape[1]
        vw = v_ref.shape[-1]
        s_w[...] = _diff_scores(q_ref[0], k_ref[0])
        s = s_r[...]
        a_w[...] = jnp.exp2(s - jnp.max(s, axis=-1, keepdims=True)).astype(BF16)
        oa = jnp.dot(a_r[...], va_ref[...], preferred_element_type=F32)
        o = oa[0:rows, 0:vw] / oa[0:rows, vw:vw + 1] - lam * (oa[rows:, 0:vw] / oa[rows:, vw:vw + 1])
        o_ref[0] = _diff_finish(o, sg_ref, lam_init)

    @pl.when(n % 2 == 0)
    def _():
        stages(s0, s1, a1, a0)

    @pl.when(n % 2 == 1)
    def _():
        stages(s1, s0, a0, a1)


def _diff_ctx_kernel(q_ref, k_ref, v_ref, lam_ref, sg_ref, o_ref, *, lam_init):
    a =_diff_weights(_diff_scores(q_ref[0], k_ref[0]), _diff_lambda(lam_ref, lam_init))
    o_ref[0] = _diff_finish(jnp.dot(a, v_ref[0], preferred_element_type=F32), sg_ref, lam_init)


def _diff_attn(q, k, v, lam_p, subln_g, lam_init, n_lat, n_ctx):
    n_batch, t_all, d = q.shape
    hw = 2 * HEAD_DIM
    n_heads = d // hw
    tq = TQ
    nq = n_lat // tq
    n_items = n_batch * n_heads * nq
    qk = lambda n: _pipe_item(n, n_items, 0, n_heads, nq)
    pv = lambda n: _pipe_item(n, n_items, 2, n_heads, nq)
    att = pl.pallas_call(
        functools.partial(_diff_lat_kernel, lam_init=lam_init, n_items=n_items, nq=nq, n_heads=n_heads),
        grid=(n_items + 2,),
        in_specs=[pl.BlockSpec((1, tq, hw), lambda n: (qk(n)[0], qk(n)[2], qk(n)[1])),
                  pl.BlockSpec((1, t_all, hw), lambda n: (qk(n)[0], 0, qk(n)[1])),
                  pl.BlockSpec((1, t_all, hw), lambda n: (pv(n)[0], 0, pv(n)[1])),
                  pl.BlockSpec(lam_p.shape, lambda n: (0, 0)),
                  pl.BlockSpec((1, hw), lambda n: (0, 0))],
        out_specs=pl.BlockSpec((1, tq, hw), lambda n: (pv(n)[0], pv(n)[2], pv(n)[1])),
        out_shape=jax.ShapeDtypeStruct((n_batch, n_lat, d), BF16),
        scratch_shapes=[pltpu.VMEM((t_all, 2 * hw), BF16),
                        pltpu.VMEM((2 * tq, t_all), F32), pltpu.VMEM((2 * tq, t_all), F32),
                        pltpu.VMEM((2 * tq, t_all), BF16), pltpu.VMEM((2 * tq, t_all), BF16)],
        compiler_params=_params(("arbitrary",)),
        name="diff_lat",
    )(q, k, v, lam_p, subln_g)
    cblk = n_lat // n_ctx
    full2 = lambda b, h, i: (0, 0)
    att_ctx = pl.pallas_call(
        functools.partial(_diff_ctx_kernel, lam_init=lam_init),
        grid=(n_batch, n_heads, n_ctx // TQ),
        in_specs=[pl.BlockSpec((1, TQ, hw), lambda b, h, i: (b, n_lat // TQ + i, h)),
                  pl.BlockSpec((1, n_ctx, hw), lambda b, h, i: (b, cblk, h)),
                  pl.BlockSpec((1, n_ctx, hw), lambda b, h, i: (b, cblk, h)),
                  pl.BlockSpec(lam_p.shape, full2),
                  pl.BlockSpec((1, hw), full2)],
        out_specs=pl.BlockSpec((1, TQ, hw), lambda b, h, i: (b, i, h)),
        out_shape=jax.ShapeDtypeStruct((n_batch, n_ctx, d), BF16),
        compiler_params=_params(("arbitrary", "arbitrary", "arbitrary")),
        name="diff_ctx",
    )(q, k, v, lam_p, subln_g)
    return jnp.concatenate([att, att_ctx], axis=1)


def _out_kernel(*refs, tt, n_lat, n_parts):
    x_ref, modl_ref, modc_ref = refs[0:3]
    parts = refs[3:3 + n_parts]
    w_ref = refs[3 + n_parts]
    o_ref = refs[4 + n_parts]
    d = x_ref.shape[-1]
    is_ctx = _is_ctx_rows(pl.program_id(1), tt, n_lat)
    gate = _mod_pick(is_ctx, modl_ref, modc_ref, 2, d)
    y = jnp.zeros((tt, d), F32)
    row = 0
    for p in parts:
        kp = p.shape[-1]
        y = y + jnp.dot(p[0], w_ref[row:row + kp, :], preferred_element_type=F32)
        row += kp
    o_ref[0] = x_ref[0] + gate * y


def _out_proj(xs, mod3, layer, parts, w_out, n_lat):
    n_batch, t_all, d = xs.shape
    tt = _pick_tile(t_all, 768)
    tok = lambda b, t: (b, t, 0)
    return pl.pallas_call(
        functools.partial(_out_kernel, tt=tt, n_lat=n_lat, n_parts=len(parts)),
        grid=(n_batch, t_all // tt),
        in_specs=[pl.BlockSpec((1, tt, d), tok)] + _mod_specs(layer, n_batch, 6 * d, 2)
        + [pl.BlockSpec((1, tt, p.shape[-1]), tok) for p in parts]
        + [pl.BlockSpec(w_out.shape, lambda b, t: (0, 0))],
        out_specs=pl.BlockSpec((1, tt, d), tok),
        out_shape=jax.ShapeDtypeStruct(xs.shape, F32),
        compiler_params=_params(("arbitrary", "arbitrary")),
        name="out_proj",
    )(xs, mod3, mod3, *parts, w_out)


def _ffn_kernel(x_ref, modl_ref, modc_ref, g_ref, wg_ref, wu_ref, wd_ref, o_ref, h_ref, acc_ref, *, tt, n_lat):
    f = pl.program_id(2)
    d = x_ref.shape[-1]
    is_ctx = _is_ctx_rows(pl.program_id(1), tt, n_lat)

    @pl.when(f == 0)
    def _():
        h_ref[...] = _norm_mod(x_ref[0], g_ref[...], is_ctx, modl_ref, modc_ref, 3).astype(BF16)
        acc_ref[...] = jnp.zeros_like(acc_ref)

    h = h_ref[...]
    g = jnp.dot(h, wg_ref[...], preferred_element_type=F32)
    u = jnp.dot(h, wu_ref[...], preferred_element_type=F32)
    act = (g * jax.nn.sigmoid(g) * u).astype(BF16)
    acc_ref[...] += jnp.dot(act, wd_ref[...], preferred_element_type=F32)

    @pl.when(f == pl.num_programs(2) - 1)
    def _():
        gate = _mod_pick(is_ctx, modl_ref, modc_ref, 5, d)
        o_ref[0] = x_ref[0] + gate * acc_ref[...]


def _ffn(xs, mod3, layer, g2, wg, wu, wd, n_lat):
    n_batch, t_all, d = xs.shape
    ffn = wg.shape[-1]
    tt = _pick_tile(t_all, 768)
    tf = _pick_tile(ffn, FFN_TILE)
    tok = lambda b, t, f: (b, t, 0)
    return pl.pallas_call(
        functools.partial(_ffn_kernel, tt=tt, n_lat=n_lat),
        grid=(n_batch, t_all // tt, ffn // tf),
        in_specs=[pl.BlockSpec((1, tt, d), tok)] + _mod_specs(layer, n_batch, 6 * d, 3) + [
            pl.BlockSpec((1, d), lambda b, t, f: (0, 0)),
            pl.BlockSpec((d, tf), lambda b, t, f: (0, f)),
            pl.BlockSpec((d, tf), lambda b, t, f: (0, f)),
            pl.BlockSpec((tf, d), lambda b, t, f: (f, 0))],
        out_specs=pl.BlockSpec((1, tt, d), tok),
        out_shape=jax.ShapeDtypeStruct(xs.shape, F32),
        scratch_shapes=[pltpu.VMEM((tt, d), BF16), pltpu.VMEM((tt, d), F32)],
        compiler_params=_params(("arbitrary", "arbitrary", "arbitrary")),
        name="ffn",
    )(xs, mod3, mod3, g2, wg, wu, wd)


def _top2(logits):
    lane = lax.broadcasted_iota(jnp.int32, logits.shape, 1).astype(F32)
    neg = jnp.float32(-jnp.inf)
    lg = jnp.where(lane < N_EXPERTS, logits, neg)
    m1 = jnp.max(lg, axis=-1, keepdims=True)
    i1 = jnp.min(jnp.where(lg == m1, lane, float(LANES)), axis=-1, keepdims=True)
    lg2 = jnp.where(lane == i1, neg, lg)
    m2 = jnp.max(lg2, axis=-1, keepdims=True)
    i2 = jnp.min(jnp.where(lg2 == m2, lane, float(LANES)), axis=-1, keepdims=True)
    e2 = jnp.exp(m2 - m1)
    den = 1.0 + e2
    return lane, i1, i2, 1.0 / den, e2 / den


def _route_kernel(x_ref, modl_ref, modc_ref, g_ref, rw_ref, info_ref, w_ref, cnt_ref, run_ref, *, tt, n_lat):
    @pl.when((pl.program_id(0) == 0) & (pl.program_id(1) == 0))
    def _():
        run_ref[...] = jnp.zeros_like(run_ref)

    is_ctx = _is_ctx_rows(pl.program_id(1), tt, n_lat)
    h = _norm_mod(x_ref[0], g_ref[...], is_ctx, modl_ref, modc_ref, 3)
    logits = jnp.dot(h, rw_ref[...], precision=lax.Precision.HIGHEST, preferred_element_type=F32)
    lane, i1, i2, w1, w2 = _top2(logits)
    chosen = jnp.where((lane == i1) | (lane == i2), 1.0, 0.0)
    before = (lax.broadcasted_iota(jnp.int32, (tt, tt), 1) < lax.broadcasted_iota(jnp.int32, (tt, tt), 0))
    pre = jnp.dot(jnp.where(before, 1.0, 0.0).astype(BF16), chosen.astype(BF16),
                  preferred_element_type=F32) + run_ref[...]
    r1 = jnp.sum(jnp.where(lane == i1, pre, 0.0), axis=-1, keepdims=True)
    r2 = jnp.sum(jnp.where(lane == i2, pre, 0.0), axis=-1, keepdims=True)
    run_ref[...] += jnp.sum(chosen, axis=0, keepdims=True)
    cnt_ref[...] = run_ref[...]
    info = jnp.where(lane == 0.0, i1, jnp.where(lane == 1.0, i2, jnp.where(lane == 2.0, r1,
                                                                          jnp.where(lane == 3.0, r2, 0.0))))
    info_ref[0] = info.T[0:8].astype(jnp.int32)
    w_ref[0] = jnp.where(lane == 0.0, w1, jnp.where(lane == 1.0, w2, 0.0))


def _route(xs, mod3, layer, g2, router_w, n_lat, nb, tt):
    n_batch, t_all, d = xs.shape
    rw = jnp.zeros((d, LANES), F32).at[:, :router_w.shape[1]].set(router_w)
    return pl.pallas_call(
        functools.partial(_route_kernel, tt=tt, n_lat=n_lat),
        grid=(n_batch, nb),
        in_specs=[pl.BlockSpec((1, tt, d), lambda b, j: (b, j, 0))] + _mod_specs(layer, n_batch, 6 * d, 2) + [
            pl.BlockSpec((1, d), lambda b, j: (0, 0)),
            pl.BlockSpec((d, LANES), lambda b, j: (0, 0))],
        out_specs=[pl.BlockSpec((1, 8, tt), lambda b, j: (b * nb + j, 0, 0)),
                   pl.BlockSpec((1, tt, LANES), lambda b, j: (b, j, 0)),
                   pl.BlockSpec((1, LANES), lambda b, j: (0, 0))],
        out_shape=[jax.ShapeDtypeStruct((n_batch * nb, 8, tt), jnp.int32),
                   jax.ShapeDtypeStruct((n_batch, nb * tt, LANES), F32),
                   jax.ShapeDtypeStruct((1, LANES), F32)],
        scratch_shapes=[pltpu.VMEM((1, LANES), F32)],
        compiler_params=_params(("arbitrary", "arbitrary")),
        name="moe_route",
    )(xs, mod3, mod3, g2, rw)


def _slot_row(rows_ref, k, r, p):
    return pl.multiple_of(rows_ref[k, r], p)


def _dispatch_kernel(fill_ref, nu_ref, x_ref, modl_ref, modc_ref, g_ref, rows_ref, xs_ref,
                     hbuf_ref, zero_ref, sems, zsem, *, tt, n_lat, tm):
    nb = pl.num_programs(1)
    step = pl.program_id(0) * nb + pl.program_id(1)
    nsteps = pl.num_programs(0) * nb
    slot = lax.rem(step, 2)
    p = x_ref.shape[-1] // LANES

    def fill_copy(e):
        return pltpu.make_async_copy(zero_ref, xs_ref.at[pl.ds(pl.multiple_of(fill_ref[e] * p, p), tm * p)], zsem)

    @pl.when(step == 0)
    def _():
        zero_ref[...] = jnp.zeros_like(zero_ref)
        for e in range(N_EXPERTS):
            fill_copy(e).start()
        for e in range(N_EXPERTS):
            fill_copy(e).wait()

        def clear_tile(t, carry):
            cp = pltpu.make_async_copy(
                zero_ref, xs_ref.at[pl.ds(pl.multiple_of(t * (tm * p), tm * p), tm * p)], zsem)
            cp.start()
            cp.wait()
            return carry

        lax.fori_loop(nu_ref[0], xs_ref.shape[0] // (tm * p), clear_tile, 0)

    is_ctx = _is_ctx_rows(pl.program_id(1), tt, n_lat)
    h = _norm_mod(x_ref[0], g_ref[...], is_ctx, modl_ref, modc_ref, 3)
    for s in range(p):
        hbuf_ref[slot, pl.ds(s, tt, stride=p), :] = h[:, s * LANES:(s + 1) * LANES]

    def issue(r, carry):
        src = hbuf_ref.at[slot, pl.ds(pl.multiple_of(r * p, p), p)]
        for k in range(2):
            pltpu.make_async_copy(src, xs_ref.at[pl.ds(_slot_row(rows_ref, k, r, p), p)],
                                  sems.at[slot]).start()
        return carry

    lax.fori_loop(0, tt, issue, 0, unroll=8)

    def wait_rows(sl):
        for _ in range(2):
            pltpu.make_async_copy(hbuf_ref.at[sl], xs_ref.at[pl.ds(0, tt * p)], sems.at[sl]).wait()

    @pl.when(step > 0)
    def _():
        wait_rows(1 - slot)

    @pl.when(step == nsteps - 1)
    def _():
        wait_rows(slot)


def _dispatch(xs, mod3, layer, g2, slot_rows, fill, n_used, n_lat, nb, tt, tm, n_rows):
    n_batch, _, d = xs.shape
    p = d // LANES
    grid_spec = pltpu.PrefetchScalarGridSpec(
        num_scalar_prefetch=2,
        grid=(n_batch, nb),
        in_specs=[pl.BlockSpec((1, tt, d), lambda b, j, *_: (b, j, 0)),
                  pl.BlockSpec((None, 1, 6 * d), lambda b, j, *_: (layer * MOD_ROWS + b, 0, 0)),
                  pl.BlockSpec((None, 1, 6 * d), lambda b, j, *_: (layer * MOD_ROWS + n_batch, 0, 0)),
                  pl.BlockSpec((1, d), lambda b, j, *_: (0, 0)),
                  pl.BlockSpec((None, 2, tt), lambda b, j, *_: (b * nb + j, 0, 0), memory_space=pltpu.SMEM)],
        out_specs=pl.BlockSpec(memory_space=pl.ANY),
        scratch_shapes=[pltpu.VMEM((2, tt * p, LANES), F32), pltpu.VMEM((tm * p, LANES), F32),
                        pltpu.SemaphoreType.DMA((2,)), pltpu.SemaphoreType.DMA(())],
    )
    return pl.pallas_call(
        functools.partial(_dispatch_kernel, tt=tt, n_lat=n_lat, tm=tm),
        grid_spec=grid_spec,
        out_shape=jax.ShapeDtypeStruct((n_rows * p, LANES), F32),
        compiler_params=_params(("arbitrary", "arbitrary")),
        name="moe_dispatch",
    )(fill, n_used, xs, mod3, mod3, g2, slot_rows)


def _experts_kernel(te_ref, nu_ref, xs_ref, wg_ref, wu_ref, wd_ref, o_ref, h_ref, acc_ref, *, tm):
    t = pl.program_id(0)
    f = pl.program_id(1)
    p = o_ref.shape[0] // tm
    live = t < nu_ref[0]

    @pl.when(live & (f == 0))
    def _():
        h_ref[...] = jnp.concatenate([xs_ref[pl.ds(s, tm, stride=p), :] for s in range(p)],
                                     axis=-1).astype(BF16)
        acc_ref[...] = jnp.zeros_like(acc_ref)

    @pl.when(live)
    def _():
        h = h_ref[...]
        g = jnp.dot(h, wg_ref[0], preferred_element_type=F32)
        u = jnp.dot(h, wu_ref[0], preferred_element_type=F32)
        act = (g * jax.nn.sigmoid(g) * u).astype(BF16)
        acc_ref[...] += jnp.dot(act, wd_ref[0], preferred_element_type=F32)

    @pl.when(live & (f == pl.num_programs(1) - 1))
    def _():
        for s in range(p):
            o_ref[pl.ds(s, tm, stride=p), :] = acc_ref[:, s * LANES:(s + 1) * LANES]

    @pl.when(jnp.logical_not(live) & (f == pl.num_programs(1) - 1))
    def _():
        o_ref[...] = jnp.zeros_like(o_ref)


def _experts(xsort, tile_expert, n_used, wg, wu, wd, tm, n_tiles):
    n_exp, d, ffn = wg.shape
    p = d // LANES
    tf = _pick_tile(ffn, FFN_TILE)
    nf = ffn // tf

    def fcol(t, f, te, nu):
        return jnp.where(t < nu[0], f, nf - 1)

    grid_spec = pltpu.PrefetchScalarGridSpec(
        num_scalar_prefetch=2,
        grid=(n_tiles, nf),
        in_specs=[pl.BlockSpec((tm * p, LANES), lambda t, f, te, nu: (t, 0)),
                  pl.BlockSpec((1, d, tf), lambda t, f, te, nu: (te[t], 0, fcol(t, f, te, nu))),
                  pl.BlockSpec((1, d, tf), lambda t, f, te, nu: (te[t], 0, fcol(t, f, te, nu))),
                  pl.BlockSpec((1, tf, d), lambda t, f, te, nu: (te[t], fcol(t, f, te, nu), 0))],
        out_specs=pl.BlockSpec((tm * p, LANES), lambda t, f, te, nu: (t, 0)),
        scratch_shapes=[pltpu.VMEM((tm, d), BF16), pltpu.VMEM((tm, d), F32)],
    )
    return pl.pallas_call(
        functools.partial(_experts_kernel, tm=tm),
        grid_spec=grid_spec,
        out_shape=jax.ShapeDtypeStruct((n_tiles * tm * p, LANES), F32),
        compiler_params=_params(("arbitrary", "arbitrary")),
        name="moe_experts",
    )(tile_expert, n_used, xsort, wg, wu, wd)


def _combine_kernel(x_ref, modl_ref, modc_ref, w_ref, info0_ref, infon_ref, y_ref, fg_ref,
                    o_ref, buf1_ref, buf2_ref, sems, *, tt, n_lat, final):
    nb = pl.num_programs(1)
    step = pl.program_id(0) * nb + pl.program_id(1)
    nsteps = pl.num_programs(0) * nb
    slot = lax.rem(step, 2)
    d = x_ref.shape[-1]
    p = d // LANES
    bufs = (buf1_ref, buf2_ref)

    def issue_block(info_ref, sl):
        def issue(r, carry):
            for k in range(2):
                pltpu.make_async_copy(y_ref.at[pl.ds(_slot_row(info_ref, k, r, p), p)],
                                      bufs[k].at[sl, pl.ds(pl.multiple_of(r * p, p), p)], sems.at[sl]).start()
            return carry
        lax.fori_loop(0, tt, issue, 0, unroll=8)

    @pl.when(step == 0)
    def _():
        issue_block(info0_ref, 0)

    @pl.when(step + 1 < nsteps)
    def _():
        issue_block(infon_ref, 1 - slot)

    for k in range(2):
        pltpu.make_async_copy(y_ref.at[pl.ds(0, tt * p)], bufs[k].at[slot], sems.at[slot]).wait()

    is_ctx = _is_ctx_rows(pl.program_id(1), tt, n_lat)
    w = w_ref[0]
    w1, w2 = w[:, 0:1], w[:, 1:2]
    outs = []
    for s in range(p):
        cols = slice(s * LANES, (s + 1) * LANES)
        y = w1 * buf1_ref[slot, pl.ds(s, tt, stride=p), :] + w2 * buf2_ref[slot, pl.ds(s, tt, stride=p), :]
        gate = jnp.where(is_ctx, modc_ref[:, 5 * d + s * LANES:5 * d + (s + 1) * LANES],
                         modl_ref[:, 5 * d + s * LANES:5 * d + (s + 1) * LANES])
        xn = x_ref[0, :, cols] + gate * y
        if final:
            outs.append(xn)
        else:
            o_ref[0, :, cols] = xn
    if final:
        ss = sum(jnp.sum(xn * xn, axis=-1, keepdims=True) for xn in outs)
        r = lax.rsqrt(ss * (1.0 / d) + NORM_EPS)
        for s, xn in enumerate(outs):
            cols = slice(s * LANES, (s + 1) * LANES)
            o_ref[0, :, cols] = xn * r * fg_ref[:, cols]


def _combine(xs, mod3, layer, wts, slot_rows, y, final_g, n_lat, nb, tt, final):
    n_batch, _, d = xs.shape
    p = d // LANES
    nblk = n_batch * nb
    out_rows = nb * tt
    grid_spec = pltpu.PrefetchScalarGridSpec(
        num_scalar_prefetch=0,
        grid=(n_batch, nb),
        in_specs=[pl.BlockSpec((1, tt, d), lambda b, j, *_: (b, j, 0)),
                  pl.BlockSpec((None, 1, 6 * d), lambda b, j, *_: (layer * MOD_ROWS + b, 0, 0)),
                  pl.BlockSpec((None, 1, 6 * d), lambda b, j, *_: (layer * MOD_ROWS + n_batch, 0, 0)),
                  pl.BlockSpec((1, tt, LANES), lambda b, j, *_: (b, j, 0)),
                  pl.BlockSpec((None, 2, tt), lambda b, j, *_: (0, 0, 0), memory_space=pltpu.SMEM),
                  pl.BlockSpec((None, 2, tt), lambda b, j, *_: (jnp.minimum(b * nb + j + 1, nblk - 1), 0, 0),
                               memory_space=pltpu.SMEM),
                  pl.BlockSpec(memory_space=pl.ANY),
                  pl.BlockSpec((1, d), lambda b, j, *_: (0, 0))],
        out_specs=pl.BlockSpec((1, tt, d), lambda b, j, *_: (b, j, 0)),
        scratch_shapes=[pltpu.VMEM((2, tt * p, LANES), F32), pltpu.VMEM((2, tt * p, LANES), F32),
                        pltpu.SemaphoreType.DMA((2,))],
    )
    return pl.pallas_call(
        functools.partial(_combine_kernel, tt=tt, n_lat=n_lat, final=final),
        grid_spec=grid_spec,
        out_shape=jax.ShapeDtypeStruct((n_batch, out_rows, d), F32),
        compiler_params=_params(("arbitrary", "arbitrary")),
        name="moe_combine",
    )(xs, mod3, mod3, wts, slot_rows, slot_rows, y, final_g)


def _moe(xs, mod3, layer, g2, router_w, wg, wu, wd, final_g, n_lat, final):
    n_batch, t_all, d = xs.shape
    n_exp = wg.shape[0]
    tt = TQ
    tm = 512
    nb = (n_lat if final else t_all) // tt
    n_assign = 2 * n_batch * nb * tt
    n_tiles = -(-n_assign // tm) + n_exp
    info, wts, cnt = _route(xs, mod3, layer, g2, router_w, n_lat, nb, tt)
    counts = cnt[0, :n_exp].astype(jnp.int32)
    tiles_per = (counts + tm - 1) // tm
    tile_end = jnp.cumsum(tiles_per)
    starts = (tile_end - tiles_per) * tm
    n_used = tile_end[-1:]
    tile_ids = jnp.arange(n_tiles, dtype=jnp.int32)
    tile_expert = jnp.minimum(jnp.sum(tile_ids[:, None] >= tile_end[None, :], axis=1), n_exp - 1).astype(jnp.int32)
    expert, rank = info[:, 0:2, :], info[:, 2:4, :]
    group_start = jnp.sum(jnp.where(expert[..., None] == jnp.arange(n_exp, dtype=jnp.int32), starts, 0), axis=-1)
    slot_rows = (group_start + rank) * (d // LANES)
    xsort = _dispatch(xs, mod3, layer, g2, slot_rows, starts + counts, n_used, n_lat, nb, tt, tm,
                      (n_tiles + 1) * tm)
    y = _experts(xsort, tile_expert, n_used, wg, wu, wd, tm, n_tiles)
    return _combine(xs, mod3, layer, wts, slot_rows, y, final_g, n_lat, nb, tt, final)


def _final_kernel(x_ref, g_ref, o_ref):
    x = x_ref[0]
    o_ref[0] = x * lax.rsqrt(jnp.mean(x * x, axis=-1, keepdims=True) + NORM_EPS) * g_ref[...]


def _final_norm(xs, g, n_lat):
    n_batch, _, d = xs.shape
    tt = _pick_tile(n_lat, 256)
    return pl.pallas_call(
        _final_kernel,
        grid=(n_batch, n_lat // tt),
        in_specs=[pl.BlockSpec((1, tt, d), lambda b, t: (b, t, 0)), pl.BlockSpec((1, d), lambda b, t: (0, 0))],
        out_specs=pl.BlockSpec((1, tt, d), lambda b, t: (b, t, 0)),
        out_shape=jax.ShapeDtypeStruct((n_batch, n_lat, d), F32),
        compiler_params=_params(("arbitrary", "arbitrary")),
        name="final_norm",
    )(xs, g)


def kernel(x, c, ctx, c_ctx, ada_w, ada_b, norm1_g, norm2_g, ev_w_in, ev_conv_w, ev_ln_g, ev_ln_b,
           ev_q_norm_g, ev_k_norm_g, ev_w_out, ev_ffn_wg, ev_ffn_wu, ev_ffn_wd, od_w_in, od_lam,
           od_subln_g, od_w_out, od_router_w, od_moe_wg, od_moe_wu, od_moe_wd, final_norm_g):
    n_batch, n_lat, d = x.shape
    n_ctx = ctx.shape[1]
    depth = ada_w.shape[0]
    assert n_batch + 1 <= MOD_ROWS and n_lat % n_ctx == 0 and n_ctx % TQ == 0 and d % LANES == 0

    cond = jnp.zeros((MOD_ROWS, d), F32).at[:n_batch].set(c).at[n_batch].set(c_ctx)
    mod3 = _adaln(cond, ada_w, ada_b).reshape(depth * MOD_ROWS, 1, 6 * d)
    tabs = _rope_tables(n_lat, n_ctx)
    hs = _head_sum_matrix()
    xs = jnp.concatenate([x, ctx], axis=1)

    for layer in range(depth):
        i = layer // 2
        g1 = norm1_g[layer].reshape(1, d)
        g2 = norm2_g[layer].reshape(1, d)
        if layer % 2 == 0:
            rep = LANES // HEAD_DIM
            qg = (jnp.tile(ev_q_norm_g[i], rep) * ATT_SCALE).reshape(1, LANES)
            kg = jnp.tile(ev_k_norm_g[i], rep).reshape(1, LANES)
            ag, q, k, v = _even_in(xs, mod3, layer, g1, ev_w_in[i].astype(BF16), qg, kg, tabs, hs, n_lat)
            att = _gqa(q, k, v, n_lat, n_ctx)
            cv = _conv(ag, ev_conv_w[i], ev_ln_g[i].reshape(1, -1), ev_ln_b[i].reshape(1, -1), n_lat, n_ctx)
            xs = _out_proj(xs, mod3, layer, [cv, att], ev_w_out[i].astype(BF16), n_lat)
            xs = _ffn(xs, mod3, layer, g2, ev_ffn_wg[i].astype(BF16), ev_ffn_wu[i].astype(BF16),
                      ev_ffn_wd[i].astype(BF16), n_lat)
        else:
            lam_init = 0.8 - 0.6 * math.exp(-0.3 * layer)
            q, k, v = _odd_in(xs, mod3, layer, g1, od_w_in[i].astype(BF16), tabs, n_lat)
            att = _diff_attn(q, k, v, od_lam[i], od_subln_g[i].reshape(1, -1), lam_init, n_lat, n_ctx)
            xs = _out_proj(xs, mod3, layer, [att], od_w_out[i].astype(BF16), n_lat)
            xs = _moe(xs, mod3, layer, g2, od_router_w[i], od_moe_wg[i].astype(BF16),
                      od_moe_wu[i].astype(BF16), od_moe_wd[i].astype(BF16), final_norm_g.reshape(1, d),
                      n_lat, final=layer == depth - 1)
    if depth % 2 == 0:
        return xs
    return _final_norm(xs, final_norm_g.reshape(1, d), n_lat)
```

```python
import functools
import math

import jax
import jax.numpy as jnp
from jax import lax
from jax.experimental import pallas as pl
from jax.experimental.pallas import tpu as pltpu

F32 = jnp.float32
BF16 = jnp.bfloat16

HEAD_DIM = 64
GRID_W = 64
ROPE_BASE = 10000.0
NORM_EPS = 1e-6
LN_EPS = 1e-5
CONV_WIDTH = 31
N_EXPERTS = 8
LANES = 128
MOD_ROWS = 24
VMEM_LIMIT = 56 * 1024 * 1024
ATT_SCALE = HEAD_DIM ** -0.5 * math.log2(math.e)
TQ = 256
FFN_TILE = 512
CONV_CHUNK = 64


def _params(sem):
    return pltpu.CompilerParams(dimension_semantics=sem, vmem_limit_bytes=VMEM_LIMIT)


def _pick_tile(total, target):
    best = 8
    for cand in range(8, min(total, target) + 1, 8):
        if total % cand == 0:
            best = cand
    return best


def _adaln_kernel(s_ref, w_ref, b_ref, o_ref):
    s = s_ref[...]
    s = s * jax.nn.sigmoid(s)
    o_ref[0] = jnp.dot(s, w_ref[0], precision=lax.Precision.HIGHEST,
                       preferred_element_type=F32) + b_ref[0]


def _adaln(cond, ada_w, ada_b):
    depth, d, six_d = ada_w.shape
    tn = six_d // 4
    return pl.pallas_call(
        _adaln_kernel,
        grid=(depth, six_d // tn),
        in_specs=[pl.BlockSpec((MOD_ROWS, d), lambda l, n: (0, 0)),
                  pl.BlockSpec((1, d, tn), lambda l, n: (l, 0, n)),
                  pl.BlockSpec((1, 1, tn), lambda l, n: (l, 0, n))],
        out_specs=pl.BlockSpec((1, MOD_ROWS, tn), lambda l, n: (l, 0, n)),
        out_shape=jax.ShapeDtypeStruct((depth, MOD_ROWS, six_d), F32),
        compiler_params=_params(("arbitrary", "arbitrary")),
        name="adaln",
    )(cond, ada_w, ada_b.reshape(depth, 1, six_d))


def _is_ctx_rows(t, tt, n_lat):
    rows = t * tt + lax.broadcasted_iota(jnp.int32, (tt, 1), 0)
    return rows >= n_lat


def _mod_pick(is_ctx, modl_ref, modc_ref, idx, d):
    return jnp.where(is_ctx, modc_ref[:, idx * d:(idx + 1) * d], modl_ref[:, idx * d:(idx + 1) * d])


def _norm_mod(x, g, is_ctx, modl_ref, modc_ref, shift_idx):
    d = x.shape[-1]
    r = lax.rsqrt(jnp.mean(x * x, axis=-1, keepdims=True) + NORM_EPS)
    shift = _mod_pick(is_ctx, modl_ref, modc_ref, shift_idx, d)
    scale = _mod_pick(is_ctx, modl_ref, modc_ref, shift_idx + 1, d)
    return (x * r * g) * (1.0 + scale) + shift


def _rope(xs, cos, sin_a, sin_b):
    return xs * cos + pltpu.roll(xs, LANES - 16, 1) * sin_a + pltpu.roll(xs, 16, 1) * sin_b


def _rope_tables(n_lat, n_ctx):
    rows = n_lat // GRID_W
    r = jnp.broadcast_to(jnp.arange(rows, dtype=F32)[:, None], (rows, GRID_W)).reshape(-1)
    col = jnp.broadcast_to(jnp.arange(GRID_W, dtype=F32)[None, :], (rows, GRID_W)).reshape(-1)
    quarter = HEAD_DIM // 4
    inv_freq = ROPE_BASE ** (-jnp.arange(quarter, dtype=F32) / quarter)
    ang_r = r[:, None] * inv_freq
    ang_c = col[:, None] * inv_freq
    zeros = jnp.zeros_like(ang_r)
    cos64 = jnp.concatenate([jnp.cos(ang_r), jnp.cos(ang_r), jnp.cos(ang_c), jnp.cos(ang_c)], axis=1)
    sin_a64 = jnp.concatenate([-jnp.sin(ang_r), zeros, -jnp.sin(ang_c), zeros], axis=1)
    sin_b64 = jnp.concatenate([zeros, jnp.sin(ang_r), zeros, jnp.sin(ang_c)], axis=1)

    def full(tab, ctx_val):
        tab = jnp.concatenate([tab, jnp.full((n_ctx, HEAD_DIM), ctx_val, F32)], axis=0)
        return jnp.tile(tab, (1, LANES // HEAD_DIM))

    return full(cos64, 1.0), full(sin_a64, 0.0), full(sin_b64, 0.0)


def _head_sum_matrix():
    i = jnp.arange(LANES) // HEAD_DIM
    return (i[:, None] == i[None, :]).astype(BF16)


def _mod_specs(layer, n_batch, six_d, nargs):
    if nargs == 2:
        lat = lambda b, t: (layer * MOD_ROWS + b, 0, 0)
        ctx = lambda b, t: (layer * MOD_ROWS + n_batch, 0, 0)
    elif nargs == 3:
        lat = lambda b, t, f: (layer * MOD_ROWS + b, 0, 0)
        ctx = lambda b, t, f: (layer * MOD_ROWS + n_batch, 0, 0)
    else:
        lat = lambda b, t, e, f: (layer * MOD_ROWS + b, 0, 0)
        ctx = lambda b, t, e, f: (layer * MOD_ROWS + n_batch, 0, 0)
    return [pl.BlockSpec((None, 1, six_d), lat), pl.BlockSpec((None, 1, six_d), ctx)]


def _even_in_kernel(x_ref, modl_ref, modc_ref, g_ref, w_ref, qg_ref, kg_ref, cos_ref, sa_ref, sb_ref,
                    hs_ref, ag_ref, q_ref, k_ref, v_ref, *, tt, n_lat):
    d = x_ref.shape[-1]
    conv2 = d
    qw = d // 2
    kvw = (w_ref.shape[1] - conv2 - qw) // 2
    is_ctx = _is_ctx_rows(pl.program_id(1), tt, n_lat)
    h = _norm_mod(x_ref[0], g_ref[...], is_ctx, modl_ref, modc_ref, 0).astype(BF16)
    half = conv2 // 2
    for c in range(2):
        ag_ref[0, :, c * half:(c + 1) * half] = jnp.dot(
            h, w_ref[:, c * half:(c + 1) * half], preferred_element_type=F32)
    cos, sa, sb = cos_ref[...], sa_ref[...], sb_ref[...]
    hs = hs_ref[...]

    def head_norm_rope(ys, gain):
        ss = jnp.dot((ys * ys).astype(BF16), hs, preferred_element_type=F32)
        yn = ys * lax.rsqrt(ss * (1.0 / HEAD_DIM) + NORM_EPS) * gain
        return _rope(yn, cos, sa, sb).astype(BF16)

    q = jnp.dot(h, w_ref[:, conv2:conv2 + qw], preferred_element_type=F32)
    for c in range(qw // LANES):
        q_ref[0, :, c * LANES:(c + 1) * LANES] = head_norm_rope(q[:, c * LANES:(c + 1) * LANES], qg_ref[...])
    kv = jnp.dot(h, w_ref[:, conv2 + qw:], preferred_element_type=F32)
    for c in range(kvw // LANES):
        k_ref[0, :, c * LANES:(c + 1) * LANES] = head_norm_rope(kv[:, c * LANES:(c + 1) * LANES], kg_ref[...])
    v_ref[0] = kv[:, kvw:].astype(BF16)


def _even_in(xs, mod3, layer, g1, w_in, qg, kg, tabs, hs, n_lat):
    n_batch, t_all, d = xs.shape
    tt = _pick_tile(t_all, 768)
    qw = d // 2
    kvw = (w_in.shape[1] - d - qw) // 2
    full2 = lambda b, t: (0, 0)
    tok = lambda b, t: (b, t, 0)
    tab_spec = pl.BlockSpec((tt, LANES), lambda b, t: (t, 0))
    return pl.pallas_call(
        functools.partial(_even_in_kernel, tt=tt, n_lat=n_lat),
        grid=(n_batch, t_all // tt),
        in_specs=[pl.BlockSpec((1, tt, d), tok)] + _mod_specs(layer, n_batch, 6 * d, 2) + [
            pl.BlockSpec((1, d), full2),
            pl.BlockSpec(w_in.shape, full2),
            pl.BlockSpec((1, LANES), full2), pl.BlockSpec((1, LANES), full2),
            tab_spec, tab_spec, tab_spec,
            pl.BlockSpec((LANES, LANES), full2)],
        out_specs=[pl.BlockSpec((1, tt, d), tok), pl.BlockSpec((1, tt, qw), tok),
                   pl.BlockSpec((1, tt, kvw), tok), pl.BlockSpec((1, tt, kvw), tok)],
        out_shape=[jax.ShapeDtypeStruct((n_batch, t_all, d), F32),
                   jax.ShapeDtypeStruct((n_batch, t_all, qw), BF16),
                   jax.ShapeDtypeStruct((n_batch, t_all, kvw), BF16),
                   jax.ShapeDtypeStruct((n_batch, t_all, kvw), BF16)],
        compiler_params=_params(("arbitrary", "arbitrary")),
        name="even_in",
    )(xs, mod3, mod3, g1, w_in, qg, kg, *tabs, hs)


def _pipe_item(n, n_items, lag, n_mid, nq):
    j = jnp.clip(n - lag, 0, n_items - 1)
    return j // (n_mid * nq), (j // nq) % n_mid, j % nq


def _gqa_replicate(src_ref, dst_ref, kv):
    a = src_ref[0].astype(F32)
    lane = lax.broadcasted_iota(jnp.int32, a.shape, 1)
    keep = jnp.logical_xor(lane < HEAD_DIM, kv == 1)
    rep = jnp.where(keep, a, pltpu.roll(a, HEAD_DIM, 1)).astype(BF16)
    for c in range(dst_ref.shape[-1] // LANES):
        dst_ref[:, c * LANES:(c + 1) * LANES] = rep


def _gqa_stack(q):
    head = lax.broadcasted_iota(jnp.int32, q.shape, 1) // HEAD_DIM
    group = q.shape[-1] // HEAD_DIM
    return jnp.concatenate([jnp.where(head == g, q, jnp.zeros_like(q)) for g in range(group)], axis=0)


def _gqa_unstack(o, rows):
    group = o.shape[-1] // HEAD_DIM
    head = lax.broadcasted_iota(jnp.int32, (rows, o.shape[-1]), 1) // HEAD_DIM
    out = jnp.zeros((rows, o.shape[-1]), F32)
    for g in range(group):
        out = out + jnp.where(head == g, o[g * rows:(g + 1) * rows], 0.0)
    return out.astype(BF16)


def _gqa_lat_kernel(q_ref, k_ref, v_ref, o_ref, k4_ref, v4_ref, s0, s1, p0, p1, l0, l1, *, n_items, nq, n_kv):
    n = pl.program_id(0)
    _, kv_qk, qi_qk = _pipe_item(n, n_items, 0, n_kv, nq)
    _, kv_pv, qi_pv = _pipe_item(n, n_items, 2, n_kv, nq)

    @pl.when(n == 0)
    def _():
        s1[...] = jnp.zeros_like(s1)
        p0[...] = jnp.zeros_like(p0)
        l0[...] = jnp.ones_like(l0)

    @pl.when((n < n_items) & (qi_qk == 0))
    def _():
        _gqa_replicate(k_ref, k4_ref, kv_qk)

    @pl.when((n == 0) | ((n >= 2) & (qi_pv == 0)))
    def _():
        _gqa_replicate(v_ref, v4_ref, kv_pv)

    def stages(s_w, s_r, p_w, l_w, p_r, l_r):
        s_w[...] = lax.dot_general(_gqa_stack(q_ref[0]), k4_ref[...], (((1,), (1,)), ((), ())),
                                   preferred_element_type=F32)
        s = s_r[...]
        p = jnp.exp2(s - jnp.max(s, axis=-1, keepdims=True))
        l_w[...] = jnp.sum(p, axis=-1, keepdims=True)
        p_w[...] = p.astype(BF16)
        o = jnp.dot(p_r[...], v4_ref[...], preferred_element_type=F32) / l_r[...]
        o_ref[0] = _gqa_unstack(o, o_ref.shape[1])

    @pl.when(n % 2 == 0)
    def _():
        stages(s0, s1, p1, l1, p0, l0)

    @pl.when(n % 2 == 1)
    def _():
        stages(s1, s0, p0, l0, p1, l1)


def _gqa_ctx_kernel(q_ref, k_ref, v_ref, o_ref, k4_ref, v4_ref):
    _gqa_replicate(k_ref, k4_ref, pl.program_id(1))
    _gqa_replicate(v_ref, v4_ref, pl.program_id(1))
    s = lax.dot_general(_gqa_stack(q_ref[0]), k4_ref[...], (((1,), (1,)), ((), ())),
                        preferred_element_type=F32)
    p = jnp.exp2(s - jnp.max(s, axis=-1, keepdims=True))
    l = jnp.sum(p, axis=-1, keepdims=True)
    o = jnp.dot(p.astype(BF16), v4_ref[...], preferred_element_type=F32) / l
    o_ref[0] = _gqa_unstack(o, o_ref.shape[1])


def _gqa(q, k, v, n_lat, n_ctx):
    n_batch, t_all, qw = q.shape
    kvw = k.shape[-1]
    n_kv = kvw // HEAD_DIM
    gw = qw // n_kv
    group = gw // HEAD_DIM
    nq = n_lat // TQ
    n_items = n_batch * n_kv * nq
    qk = lambda n: _pipe_item(n, n_items, 0, n_kv, nq)
    pv = lambda n: _pipe_item(n, n_items, 2, n_kv, nq)
    att = pl.pallas_call(
        functools.partial(_gqa_lat_kernel, n_items=n_items, nq=nq, n_kv=n_kv),
        grid=(n_items + 2,),
        in_specs=[pl.BlockSpec((1, TQ, gw), lambda n: (qk(n)[0], qk(n)[2], qk(n)[1])),
                  pl.BlockSpec((1, t_all, kvw), lambda n: (qk(n)[0], 0, 0)),
                  pl.BlockSpec((1, t_all, kvw), lambda n: (pv(n)[0], 0, 0))],
        out_specs=pl.BlockSpec((1, TQ, gw), lambda n: (pv(n)[0], pv(n)[2], pv(n)[1])),
        out_shape=jax.ShapeDtypeStruct((n_batch, n_lat, qw), BF16),
        scratch_shapes=[pltpu.VMEM((t_all, gw), BF16), pltpu.VMEM((t_all, gw), BF16),
                        pltpu.VMEM((group * TQ, t_all), F32), pltpu.VMEM((group * TQ, t_all), F32),
                        pltpu.VMEM((group * TQ, t_all), BF16), pltpu.VMEM((group * TQ, t_all), BF16),
                        pltpu.VMEM((group * TQ, 1), F32), pltpu.VMEM((group * TQ, 1), F32)],
        compiler_params=_params(("arbitrary",)),
        name="gqa_lat",
    )(q, k, v)
    cblk = n_lat // n_ctx
    att_ctx = pl.pallas_call(
        _gqa_ctx_kernel,
        grid=(n_batch, n_kv, n_ctx // TQ),
        in_specs=[pl.BlockSpec((1, TQ, gw), lambda b, h, i: (b, nq + i, h)),
                  pl.BlockSpec((1, n_ctx, kvw), lambda b, h, i: (b, cblk, 0)),
                  pl.BlockSpec((1, n_ctx, kvw), lambda b, h, i: (b, cblk, 0))],
        out_specs=pl.BlockSpec((1, TQ, gw), lambda b, h, i: (b, i, h)),
        out_shape=jax.ShapeDtypeStruct((n_batch, n_ctx, qw), BF16),
        scratch_shapes=[pltpu.VMEM((n_ctx, gw), BF16), pltpu.VMEM((n_ctx, gw), BF16)],
        compiler_params=_params(("arbitrary", "arbitrary", "arbitrary")),
        name="gqa_ctx",
    )(q, k, v)
    return jnp.concatenate([att, att_ctx], axis=1)


def _conv_kernel(ag_ref, w_ref, lng_ref, lnb_ref, o_ref, ypad_ref, *, n_lat, n_ctx):
    ch = o_ref.shape[-1]
    pad = CONV_WIDTH // 2
    halo = 16
    lat0 = halo
    ctx0 = 2 * halo + n_lat
    zeros = jnp.zeros((halo, ch), F32)
    ypad_ref[0:halo, :] = zeros
    ypad_ref[halo + n_lat:ctx0, :] = zeros
    ypad_ref[ctx0 + n_ctx:ctx0 + n_ctx + halo, :] = zeros

    def glu(src0, dst0, n):
        def body(i, carry):
            r = pl.multiple_of(i * CONV_CHUNK, CONV_CHUNK)
            a = ag_ref[0, pl.ds(src0 + r, CONV_CHUNK), 0:ch]
            g = ag_ref[0, pl.ds(src0 + r, CONV_CHUNK), ch:2 * ch]
            ypad_ref[pl.ds(dst0 + r, CONV_CHUNK), :] = a * jax.nn.sigmoid(g)
            return carry
        lax.fori_loop(0, n // CONV_CHUNK, body, 0)

    glu(0, lat0, n_lat)
    glu(n_lat, ctx0, n_ctx)

    def conv(src0, dst0, n):
        def body(i, carry):
            r = pl.multiple_of(i * CONV_CHUNK, CONV_CHUNK)
            accs = []
            for c in range(ch // LANES):
                cols = slice(c * LANES, (c + 1) * LANES)
                win = ypad_ref[pl.ds(src0 - halo + r, CONV_CHUNK + 2 * halo), cols]
                acc = jnp.zeros((CONV_CHUNK, LANES), F32)
                for s in range(8):
                    taps = [j for j in range(CONV_WIDTH) if (j + halo - pad) % 8 == s]
                    if not taps:
                        continue
                    sh = win[s:s + CONV_CHUNK + 2 * halo - 8]
                    for j in taps:
                        a8 = (j + halo - pad) - s
                        acc = acc + w_ref[j:j + 1, cols] * sh[a8:a8 + CONV_CHUNK]
                accs.append(acc)
            mu = sum(jnp.sum(a, axis=-1, keepdims=True) for a in accs) * (1.0 / ch)
            cens = [a - mu for a in accs]
            var = sum(jnp.sum(cn * cn, axis=-1, keepdims=True) for cn in cens) * (1.0 / ch)
            rstd = lax.rsqrt(var + LN_EPS)
            for c, cn in enumerate(cens):
                cols = slice(c * LANES, (c + 1) * LANES)
                yn = cn * rstd * lng_ref[:, cols] + lnb_ref[:, cols]
                o_ref[0, pl.ds(dst0 + r, CONV_CHUNK), cols] = (yn * jax.nn.sigmoid(yn)).astype(BF16)
            return carry
        lax.fori_loop(0, n // CONV_CHUNK, body, 0)

    conv(lat0, 0, n_lat)
    conv(ctx0, n_lat, n_ctx)


def _conv(ag, conv_w, ln_g, ln_b, n_lat, n_ctx):
    n_batch, t_all, two_ch = ag.shape
    ch = two_ch // 2
    full2 = lambda b: (0, 0)
    return pl.pallas_call(
        functools.partial(_conv_kernel, n_lat=n_lat, n_ctx=n_ctx),
        grid=(n_batch,),
        in_specs=[pl.BlockSpec((1, t_all, two_ch), lambda b: (b, 0, 0)),
                  pl.BlockSpec(conv_w.shape, full2),
                  pl.BlockSpec((1, ch), full2), pl.BlockSpec((1, ch), full2)],
        out_specs=pl.BlockSpec((1, t_all, ch), lambda b: (b, 0, 0)),
        out_shape=jax.ShapeDtypeStruct((n_batch, t_all, ch), BF16),
        scratch_shapes=[pltpu.VMEM((t_all + 48, ch), F32)],
        compiler_params=_params(("arbitrary",)),
        name="conv",
    )(ag, conv_w, ln_g, ln_b)


def _odd_in_kernel(x_ref, modl_ref, modc_ref, g_ref, w_ref, cos_ref, sa_ref, sb_ref,
                   q_ref, k_ref, v_ref, *, tt, n_lat):
    d = x_ref.shape[-1]
    is_ctx = _is_ctx_rows(pl.program_id(1), tt, n_lat)
    h = _norm_mod(x_ref[0], g_ref[...], is_ctx, modl_ref, modc_ref, 0).astype(BF16)
    cos, sa, sb = cos_ref[...], sa_ref[...], sb_ref[...]
    cw = 512
    for part, (dst, scale) in enumerate(((q_ref, ATT_SCALE), (k_ref, None), (v_ref, None))):
        for c0 in range(0, d, cw):
            y = jnp.dot(h, w_ref[:, part * d + c0:part * d + c0 + cw], preferred_element_type=F32)
            if part == 2:
                dst[0, :, c0:c0 + cw] = y.astype(BF16)
                continue
            for c in range(cw // LANES):
                ys = _rope(y[:, c * LANES:(c + 1) * LANES], cos, sa, sb)
                if scale is not None:
                    ys = ys * scale
                dst[0, :, c0 + c * LANES:c0 + (c + 1) * LANES] = ys.astype(BF16)


def _odd_in(xs, mod3, layer, g1, w_in, tabs, n_lat):
    n_batch, t_all, d = xs.shape
    tt = _pick_tile(t_all, 768)
    full2 = lambda b, t: (0, 0)
    tok = lambda b, t: (b, t, 0)
    tab_spec = pl.BlockSpec((tt, LANES), lambda b, t: (t, 0))
    return pl.pallas_call(
        functools.partial(_odd_in_kernel, tt=tt, n_lat=n_lat),
        grid=(n_batch, t_all // tt),
        in_specs=[pl.BlockSpec((1, tt, d), tok)] + _mod_specs(layer, n_batch, 6 * d, 2) + [
            pl.BlockSpec((1, d), full2),
            pl.BlockSpec(w_in.shape, full2),
            tab_spec, tab_spec, tab_spec],
        out_specs=[pl.BlockSpec((1, tt, d), tok)] * 3,
        out_shape=[jax.ShapeDtypeStruct((n_batch, t_all, d), BF16)] * 3,
        compiler_params=_params(("arbitrary", "arbitrary")),
        name="odd_in",
    )(xs, mod3, mod3, g1, w_in, *tabs)


def _diff_lambda(lam_ref, lam_init):
    lp = lam_ref[...]
    return (jnp.exp(jnp.sum(lp[0:1] * lp[1:2], axis=-1, keepdims=True))
            - jnp.exp(jnp.sum(lp[2:3] * lp[3:4], axis=-1, keepdims=True)) + lam_init)


def _diff_scores(q, k):
    first = lax.broadcasted_iota(jnp.int32, q.shape, 1) < HEAD_DIM
    zero = jnp.zeros_like(q)
    qs = jnp.concatenate([jnp.where(first, q, zero), jnp.where(first, zero, q)], axis=0)
    return lax.dot_general(qs, k, (((1,), (1,)), ((), ())), preferred_element_type=F32)


def _diff_weights(s, lam):
    rows = s.shape[0] // 2
    p = jnp.exp2(s - jnp.max(s, axis=-1, keepdims=True))
    p = p / jnp.sum(p, axis=-1, keepdims=True)
    return (p[0:rows] - lam * p[rows:]).astype(BF16)


def _diff_finish(o, sg_ref, lam_init):
    r = lax.rsqrt(jnp.mean(o * o, axis=-1, keepdims=True) + NORM_EPS)
    return ((o * r * sg_ref[...]) * (1.0 - lam_init)).astype(BF16)


def _diff_lat_kernel(q_ref, k_ref, v_ref, lam_ref, sg_ref, o_ref, va_ref, s0, s1, a0, a1,
                     *, lam_init, n_items, nq, n_heads):
    n = pl.program_id(0)
    lam = _diff_lambda(lam_ref, lam_init)
    _, _, qi_pv = _pipe_item(n, n_items, 2, n_heads, nq)

    @pl.when(n == 0)
    def _():
        s1[...] = jnp.zeros_like(s1)
        a0[...] = jnp.ones_like(a0)
        va_ref[:, v_ref.shape[-1]:] = jnp.ones((va_ref.shape[0], va_ref.shape[1] - v_ref.shape[-1]), BF16)

    @pl.when((n == 0) | ((n >= 2) & (qi_pv == 0)))
    def _():
        va_ref[:, 0:v_ref.shape[-1]] = v_ref[0]

    def stages(s_w, s_r, a_w, a_r):
        rows = q_ref.shape[1]
        vw = v_ref.shape[-1]
        s_w[...] = _diff_scores(q_ref[0], k_ref[0])
        s = s_r[...]
        a_w[...] = jnp.exp2(s - jnp.max(s, axis=-1, keepdims=True)).astype(BF16)
        oa = jnp.dot(a_r[...], va_ref[...], preferred_element_type=F32)
        o = oa[0:rows, 0:vw] / oa[0:rows, vw:vw + 1] - lam * (oa[rows:, 0:vw] / oa[rows:, vw:vw + 1])
        o_ref[0] = _diff_finish(o, sg_ref, lam_init)

    @pl.when(n % 2 == 0)
    def _():
        stages(s0, s1, a1, a0)

    @pl.when(n % 2 == 1)
    def _():
        stages(s1, s0, a0, a1)


def _diff_ctx_kernel(q_ref, k_ref, v_ref, lam_ref, sg_ref, o_ref, *, lam_init):
    a =_diff_weights(_diff_scores(q_ref[0], k_ref[0]), _diff_lambda(lam_ref, lam_init))
    o_ref[0] = _diff_finish(jnp.dot(a, v_ref[0], preferred_element_type=F32), sg_ref, lam_init)


def _diff_attn(q, k, v, lam_p, subln_g, lam_init, n_lat, n_ctx, need_ctx):
    n_batch, t_all, d = q.shape
    hw = 2 * HEAD_DIM
    n_heads = d // hw
    tq = TQ
    nq = n_lat // tq
    n_items = n_batch * n_heads * nq
    qk = lambda n: _pipe_item(n, n_items, 0, n_heads, nq)
    pv = lambda n: _pipe_item(n, n_items, 2, n_heads, nq)
    att = pl.pallas_call(
        functools.partial(_diff_lat_kernel, lam_init=lam_init, n_items=n_items, nq=nq, n_heads=n_heads),
        grid=(n_items + 2,),
        in_specs=[pl.BlockSpec((1, tq, hw), lambda n: (qk(n)[0], qk(n)[2], qk(n)[1])),
                  pl.BlockSpec((1, t_all, hw), lambda n: (qk(n)[0], 0, qk(n)[1])),
                  pl.BlockSpec((1, t_all, hw), lambda n: (pv(n)[0], 0, pv(n)[1])),
                  pl.BlockSpec(lam_p.shape, lambda n: (0, 0)),
                  pl.BlockSpec((1, hw), lambda n: (0, 0))],
        out_specs=pl.BlockSpec((1, tq, hw), lambda n: (pv(n)[0], pv(n)[2], pv(n)[1])),
        out_shape=jax.ShapeDtypeStruct((n_batch, n_lat, d), BF16),
        scratch_shapes=[pltpu.VMEM((t_all, 2 * hw), BF16),
                        pltpu.VMEM((2 * tq, t_all), F32), pltpu.VMEM((2 * tq, t_all), F32),
                        pltpu.VMEM((2 * tq, t_all), BF16), pltpu.VMEM((2 * tq, t_all), BF16)],
        compiler_params=_params(("arbitrary",)),
        name="diff_lat",
    )(q, k, v, lam_p, subln_g)
    if not need_ctx:
        return jnp.concatenate([att, jnp.zeros((n_batch, n_ctx, d), BF16)], axis=1)
    cblk = n_lat // n_ctx
    full2 = lambda b, h, i: (0, 0)
    att_ctx = pl.pallas_call(
        functools.partial(_diff_ctx_kernel, lam_init=lam_init),
        grid=(n_batch, n_heads, n_ctx // TQ),
        in_specs=[pl.BlockSpec((1, TQ, hw), lambda b, h, i: (b, n_lat // TQ + i, h)),
                  pl.BlockSpec((1, n_ctx, hw), lambda b, h, i: (b, cblk, h)),
                  pl.BlockSpec((1, n_ctx, hw), lambda b, h, i: (b, cblk, h)),
                  pl.BlockSpec(lam_p.shape, full2),
                  pl.BlockSpec((1, hw), full2)],
        out_specs=pl.BlockSpec((1, TQ, hw), lambda b, h, i: (b, i, h)),
        out_shape=jax.ShapeDtypeStruct((n_batch, n_ctx, d), BF16),
        compiler_params=_params(("arbitrary", "arbitrary", "arbitrary")),
        name="diff_ctx",
    )(q, k, v, lam_p, subln_g)
    return jnp.concatenate([att, att_ctx], axis=1)


def _out_kernel(*refs, tt, n_lat, n_parts):
    x_ref, modl_ref, modc_ref = refs[0:3]
    parts = refs[3:3 + n_parts]
    w_ref = refs[3 + n_parts]
    o_ref = refs[4 + n_parts]
    d = x_ref.shape[-1]
    is_ctx = _is_ctx_rows(pl.program_id(1), tt, n_lat)
    gate = _mod_pick(is_ctx, modl_ref, modc_ref, 2, d)
    y = jnp.zeros((tt, d), F32)
    row = 0
    for p in parts:
        kp = p.shape[-1]
        y = y + jnp.dot(p[0], w_ref[row:row + kp, :], preferred_element_type=F32)
        row += kp
    o_ref[0] = x_ref[0] + gate * y


def _out_proj(xs, mod3, layer, parts, w_out, n_lat):
    n_batch, t_all, d = xs.shape
    tt = _pick_tile(t_all, 768)
    tok = lambda b, t: (b, t, 0)
    return pl.pallas_call(
        functools.partial(_out_kernel, tt=tt, n_lat=n_lat, n_parts=len(parts)),
        grid=(n_batch, t_all // tt),
        in_specs=[pl.BlockSpec((1, tt, d), tok)] + _mod_specs(layer, n_batch, 6 * d, 2)
        + [pl.BlockSpec((1, tt, p.shape[-1]), tok) for p in parts]
        + [pl.BlockSpec(w_out.shape, lambda b, t: (0, 0))],
        out_specs=pl.BlockSpec((1, tt, d), tok),
        out_shape=jax.ShapeDtypeStruct(xs.shape, F32),
        compiler_params=_params(("arbitrary", "arbitrary")),
        name="out_proj",
    )(xs, mod3, mod3, *parts, w_out)


def _ffn_kernel(x_ref, modl_ref, modc_ref, g_ref, wg_ref, wu_ref, wd_ref, o_ref, h_ref, acc_ref, *, tt, n_lat):
    f = pl.program_id(2)
    d = x_ref.shape[-1]
    is_ctx = _is_ctx_rows(pl.program_id(1), tt, n_lat)

    @pl.when(f == 0)
    def _():
        h_ref[...] = _norm_mod(x_ref[0], g_ref[...], is_ctx, modl_ref, modc_ref, 3).astype(BF16)
        acc_ref[...] = jnp.zeros_like(acc_ref)

    h = h_ref[...]
    g = jnp.dot(h, wg_ref[...], preferred_element_type=F32)
    u = jnp.dot(h, wu_ref[...], preferred_element_type=F32)
    act = (g * jax.nn.sigmoid(g) * u).astype(BF16)
    acc_ref[...] += jnp.dot(act, wd_ref[...], preferred_element_type=F32)

    @pl.when(f == pl.num_programs(2) - 1)
    def _():
        gate = _mod_pick(is_ctx, modl_ref, modc_ref, 5, d)
        o_ref[0] = x_ref[0] + gate * acc_ref[...]


def _ffn(xs, mod3, layer, g2, wg, wu, wd, i, n_lat):
    n_batch, t_all, d = xs.shape
    ffn = wg.shape[-1]
    tt = _pick_tile(t_all, 768)
    tf = _pick_tile(ffn, FFN_TILE)
    tok = lambda b, t, f: (b, t, 0)
    return pl.pallas_call(
        functools.partial(_ffn_kernel, tt=tt, n_lat=n_lat),
        grid=(n_batch, t_all // tt, ffn // tf),
        in_specs=[pl.BlockSpec((1, tt, d), tok)] + _mod_specs(layer, n_batch, 6 * d, 3) + [
            pl.BlockSpec((1, d), lambda b, t, f: (0, 0)),
            pl.BlockSpec((None, d, tf), lambda b, t, f: (i, 0, f)),
            pl.BlockSpec((None, d, tf), lambda b, t, f: (i, 0, f)),
            pl.BlockSpec((None, tf, d), lambda b, t, f: (i, f, 0))],
        out_specs=pl.BlockSpec((1, tt, d), tok),
        out_shape=jax.ShapeDtypeStruct(xs.shape, F32),
        scratch_shapes=[pltpu.VMEM((tt, d), BF16), pltpu.VMEM((tt, d), F32)],
        compiler_params=_params(("arbitrary", "arbitrary", "arbitrary")),
        name="ffn",
    )(xs, mod3, mod3, g2, wg, wu, wd)


def _top2(logits):
    lane = lax.broadcasted_iota(jnp.int32, logits.shape, 1).astype(F32)
    neg = jnp.float32(-jnp.inf)
    lg = jnp.where(lane < N_EXPERTS, logits, neg)
    m1 = jnp.max(lg, axis=-1, keepdims=True)
    i1 = jnp.min(jnp.where(lg == m1, lane, float(LANES)), axis=-1, keepdims=True)
    lg2 = jnp.where(lane == i1, neg, lg)
    m2 = jnp.max(lg2, axis=-1, keepdims=True)
    i2 = jnp.min(jnp.where(lg2 == m2, lane, float(LANES)), axis=-1, keepdims=True)
    e2 = jnp.exp(m2 - m1)
    den = 1.0 + e2
    return lane, i1, i2, 1.0 / den, e2 / den


def _route_kernel(x_ref, modl_ref, modc_ref, g_ref, rw_ref, info_ref, w_ref, cnt_ref, run_ref, *, tt, n_lat):
    @pl.when((pl.program_id(0) == 0) & (pl.program_id(1) == 0))
    def _():
        run_ref[...] = jnp.zeros_like(run_ref)

    is_ctx = _is_ctx_rows(pl.program_id(1), tt, n_lat)
    h = _norm_mod(x_ref[0], g_ref[...], is_ctx, modl_ref, modc_ref, 3)
    logits = jnp.dot(h, rw_ref[...], precision=lax.Precision.HIGHEST, preferred_element_type=F32)
    lane, i1, i2, w1, w2 = _top2(logits)
    chosen = jnp.where((lane == i1) | (lane == i2), 1.0, 0.0)
    before = (lax.broadcasted_iota(jnp.int32, (tt, tt), 1) < lax.broadcasted_iota(jnp.int32, (tt, tt), 0))
    pre = jnp.dot(jnp.where(before, 1.0, 0.0).astype(BF16), chosen.astype(BF16),
                  preferred_element_type=F32) + run_ref[...]
    r1 = jnp.sum(jnp.where(lane == i1, pre, 0.0), axis=-1, keepdims=True)
    r2 = jnp.sum(jnp.where(lane == i2, pre, 0.0), axis=-1, keepdims=True)
    run_ref[...] += jnp.sum(chosen, axis=0, keepdims=True)
    cnt_ref[...] = run_ref[...]
    info = jnp.where(lane == 0.0, i1, jnp.where(lane == 1.0, i2, jnp.where(lane == 2.0, r1,
                                                                          jnp.where(lane == 3.0, r2, 0.0))))
    info_ref[0] = info.T[0:8].astype(jnp.int32)
    w_ref[0] = jnp.where(lane == 0.0, w1, jnp.where(lane == 1.0, w2, 0.0))


def _route(xs, mod3, layer, g2, router_w, n_lat, nb, tt):
    n_batch, t_all, d = xs.shape
    rw = jnp.zeros((d, LANES), F32).at[:, :router_w.shape[1]].set(router_w)
    return pl.pallas_call(
        functools.partial(_route_kernel, tt=tt, n_lat=n_lat),
        grid=(n_batch, nb),
        in_specs=[pl.BlockSpec((1, tt, d), lambda b, j: (b, j, 0))] + _mod_specs(layer, n_batch, 6 * d, 2) + [
            pl.BlockSpec((1, d), lambda b, j: (0, 0)),
            pl.BlockSpec((d, LANES), lambda b, j: (0, 0))],
        out_specs=[pl.BlockSpec((1, 8, tt), lambda b, j: (b * nb + j, 0, 0)),
                   pl.BlockSpec((1, tt, LANES), lambda b, j: (b, j, 0)),
                   pl.BlockSpec((1, LANES), lambda b, j: (0, 0))],
        out_shape=[jax.ShapeDtypeStruct((n_batch * nb, 8, tt), jnp.int32),
                   jax.ShapeDtypeStruct((n_batch, nb * tt, LANES), F32),
                   jax.ShapeDtypeStruct((1, LANES), F32)],
        scratch_shapes=[pltpu.VMEM((1, LANES), F32)],
        compiler_params=_params(("arbitrary", "arbitrary")),
        name="moe_route",
    )(xs, mod3, mod3, g2, rw)


def _slot_row(rows_ref, k, r, p):
    return pl.multiple_of(rows_ref[k, r], p)


def _dispatch_kernel(fill_ref, nu_ref, x_ref, modl_ref, modc_ref, g_ref, rows_ref, xs_ref,
                     hbuf_ref, zero_ref, sems, zsem, *, tt, n_lat, tm):
    nb = pl.num_programs(1)
    step = pl.program_id(0) * nb + pl.program_id(1)
    nsteps = pl.num_programs(0) * nb
    slot = lax.rem(step, 2)
    p = x_ref.shape[-1] // LANES

    def fill_copy(e):
        return pltpu.make_async_copy(zero_ref, xs_ref.at[pl.ds(pl.multiple_of(fill_ref[e] * p, p), tm * p)], zsem)

    @pl.when(step == 0)
    def _():
        zero_ref[...] = jnp.zeros_like(zero_ref)
        for e in range(N_EXPERTS):
            fill_copy(e).start()
        for e in range(N_EXPERTS):
            fill_copy(e).wait()

        def clear_tile(t, carry):
            cp = pltpu.make_async_copy(
                zero_ref, xs_ref.at[pl.ds(pl.multiple_of(t * (tm * p), tm * p), tm * p)], zsem)
            cp.start()
            cp.wait()
            return carry

        lax.fori_loop(nu_ref[0], xs_ref.shape[0] // (tm * p), clear_tile, 0)

    is_ctx = _is_ctx_rows(pl.program_id(1), tt, n_lat)
    h = _norm_mod(x_ref[0], g_ref[...], is_ctx, modl_ref, modc_ref, 3)
    for s in range(p):
        hbuf_ref[slot, pl.ds(s, tt, stride=p), :] = h[:, s * LANES:(s + 1) * LANES]

    def issue(r, carry):
        src = hbuf_ref.at[slot, pl.ds(pl.multiple_of(r * p, p), p)]
        for k in range(2):
            pltpu.make_async_copy(src, xs_ref.at[pl.ds(_slot_row(rows_ref, k, r, p), p)],
                                  sems.at[slot]).start()
        return carry

    lax.fori_loop(0, tt, issue, 0, unroll=8)

    def wait_rows(sl):
        for _ in range(2):
            pltpu.make_async_copy(hbuf_ref.at[sl], xs_ref.at[pl.ds(0, tt * p)], sems.at[sl]).wait()

    @pl.when(step > 0)
    def _():
        wait_rows(1 - slot)

    @pl.when(step == nsteps - 1)
    def _():
        wait_rows(slot)


def _dispatch(xs, mod3, layer, g2, slot_rows, fill, n_used, n_lat, nb, tt, tm, n_rows):
    n_batch, _, d = xs.shape
    p = d // LANES
    grid_spec = pltpu.PrefetchScalarGridSpec(
        num_scalar_prefetch=2,
        grid=(n_batch, nb),
        in_specs=[pl.BlockSpec((1, tt, d), lambda b, j, *_: (b, j, 0)),
                  pl.BlockSpec((None, 1, 6 * d), lambda b, j, *_: (layer * MOD_ROWS + b, 0, 0)),
                  pl.BlockSpec((None, 1, 6 * d), lambda b, j, *_: (layer * MOD_ROWS + n_batch, 0, 0)),
                  pl.BlockSpec((1, d), lambda b, j, *_: (0, 0)),
                  pl.BlockSpec((None, 2, tt), lambda b, j, *_: (b * nb + j, 0, 0), memory_space=pltpu.SMEM)],
        out_specs=pl.BlockSpec(memory_space=pl.ANY),
        scratch_shapes=[pltpu.VMEM((2, tt * p, LANES), F32), pltpu.VMEM((tm * p, LANES), F32),
                        pltpu.SemaphoreType.DMA((2,)), pltpu.SemaphoreType.DMA(())],
    )
    return pl.pallas_call(
        functools.partial(_dispatch_kernel, tt=tt, n_lat=n_lat, tm=tm),
        grid_spec=grid_spec,
        out_shape=jax.ShapeDtypeStruct((n_rows * p, LANES), F32),
        compiler_params=_params(("arbitrary", "arbitrary")),
        name="moe_dispatch",
    )(fill, n_used, xs, mod3, mod3, g2, slot_rows)


def _experts_kernel(te_ref, nu_ref, xs_ref, wg_ref, wu_ref, wd_ref, o_ref, h_ref, acc_ref, *, tm):
    t = pl.program_id(0)
    f = pl.program_id(1)
    p = o_ref.shape[0] // tm
    live = t < nu_ref[0]

    @pl.when(live & (f == 0))
    def _():
        h_ref[...] = jnp.concatenate([xs_ref[pl.ds(s, tm, stride=p), :] for s in range(p)],
                                     axis=-1).astype(BF16)
        acc_ref[...] = jnp.zeros_like(acc_ref)

    @pl.when(live)
    def _():
        h = h_ref[...]
        g = jnp.dot(h, wg_ref[0], preferred_element_type=F32)
        u = jnp.dot(h, wu_ref[0], preferred_element_type=F32)
        act = (g * jax.nn.sigmoid(g) * u).astype(BF16)
        acc_ref[...] += jnp.dot(act, wd_ref[0], preferred_element_type=F32)

    @pl.when(live & (f == pl.num_programs(1) - 1))
    def _():
        for s in range(p):
            o_ref[pl.ds(s, tm, stride=p), :] = acc_ref[:, s * LANES:(s + 1) * LANES]

    @pl.when(jnp.logical_not(live) & (f == pl.num_programs(1) - 1))
    def _():
        o_ref[...] = jnp.zeros_like(o_ref)


def _experts(xsort, tile_expert, n_used, wg, wu, wd, i, tm, n_tiles):
    _, n_exp, d, ffn = wg.shape
    p = d // LANES
    tf = _pick_tile(ffn, FFN_TILE)
    nf = ffn // tf

    def fcol(t, f, te, nu):
        return jnp.where(t < nu[0], f, nf - 1)

    grid_spec = pltpu.PrefetchScalarGridSpec(
        num_scalar_prefetch=2,
        grid=(n_tiles, nf),
        in_specs=[pl.BlockSpec((tm * p, LANES), lambda t, f, te, nu: (t, 0)),
                  pl.BlockSpec((None, 1, d, tf), lambda t, f, te, nu: (i, te[t], 0, fcol(t, f, te, nu))),
                  pl.BlockSpec((None, 1, d, tf), lambda t, f, te, nu: (i, te[t], 0, fcol(t, f, te, nu))),
                  pl.BlockSpec((None, 1, tf, d), lambda t, f, te, nu: (i, te[t], fcol(t, f, te, nu), 0))],
        out_specs=pl.BlockSpec((tm * p, LANES), lambda t, f, te, nu: (t, 0)),
        scratch_shapes=[pltpu.VMEM((tm, d), BF16), pltpu.VMEM((tm, d), F32)],
    )
    return pl.pallas_call(
        functools.partial(_experts_kernel, tm=tm),
        grid_spec=grid_spec,
        out_shape=jax.ShapeDtypeStruct((n_tiles * tm * p, LANES), F32),
        compiler_params=_params(("arbitrary", "arbitrary")),
        name="moe_experts",
    )(tile_expert, n_used, xsort, wg, wu, wd)


def _combine_kernel(x_ref, modl_ref, modc_ref, w_ref, info0_ref, infon_ref, y_ref, fg_ref,
                    o_ref, buf1_ref, buf2_ref, sems, *, tt, n_lat, final):
    nb = pl.num_programs(1)
    step = pl.program_id(0) * nb + pl.program_id(1)
    nsteps = pl.num_programs(0) * nb
    slot = lax.rem(step, 2)
    d = x_ref.shape[-1]
    p = d // LANES
    bufs = (buf1_ref, buf2_ref)

    def issue_block(info_ref, sl):
        def issue(r, carry):
            for k in range(2):
                pltpu.make_async_copy(y_ref.at[pl.ds(_slot_row(info_ref, k, r, p), p)],
                                      bufs[k].at[sl, pl.ds(pl.multiple_of(r * p, p), p)], sems.at[sl]).start()
            return carry
        lax.fori_loop(0, tt, issue, 0, unroll=8)

    @pl.when(step == 0)
    def _():
        issue_block(info0_ref, 0)

    @pl.when(step + 1 < nsteps)
    def _():
        issue_block(infon_ref, 1 - slot)

    for k in range(2):
        pltpu.make_async_copy(y_ref.at[pl.ds(0, tt * p)], bufs[k].at[slot], sems.at[slot]).wait()

    is_ctx = _is_ctx_rows(pl.program_id(1), tt, n_lat)
    w = w_ref[0]
    w1, w2 = w[:, 0:1], w[:, 1:2]
    outs = []
    for s in range(p):
        cols = slice(s * LANES, (s + 1) * LANES)
        y = w1 * buf1_ref[slot, pl.ds(s, tt, stride=p), :] + w2 * buf2_ref[slot, pl.ds(s, tt, stride=p), :]
        gate = jnp.where(is_ctx, modc_ref[:, 5 * d + s * LANES:5 * d + (s + 1) * LANES],
                         modl_ref[:, 5 * d + s * LANES:5 * d + (s + 1) * LANES])
        xn = x_ref[0, :, cols] + gate * y
        if final:
            outs.append(xn)
        else:
            o_ref[0, :, cols] = xn
    if final:
        ss = sum(jnp.sum(xn * xn, axis=-1, keepdims=True) for xn in outs)
        r = lax.rsqrt(ss * (1.0 / d) + NORM_EPS)
        for s, xn in enumerate(outs):
            cols = slice(s * LANES, (s + 1) * LANES)
            o_ref[0, :, cols] = xn * r * fg_ref[:, cols]


def _combine(xs, mod3, layer, wts, slot_rows, y, final_g, n_lat, nb, tt, final):
    n_batch, _, d = xs.shape
    p = d // LANES
    nblk = n_batch * nb
    out_rows = nb * tt
    grid_spec = pltpu.PrefetchScalarGridSpec(
        num_scalar_prefetch=0,
        grid=(n_batch, nb),
        in_specs=[pl.BlockSpec((1, tt, d), lambda b, j, *_: (b, j, 0)),
                  pl.BlockSpec((None, 1, 6 * d), lambda b, j, *_: (layer * MOD_ROWS + b, 0, 0)),
                  pl.BlockSpec((None, 1, 6 * d), lambda b, j, *_: (layer * MOD_ROWS + n_batch, 0, 0)),
                  pl.BlockSpec((1, tt, LANES), lambda b, j, *_: (b, j, 0)),
                  pl.BlockSpec((None, 2, tt), lambda b, j, *_: (0, 0, 0), memory_space=pltpu.SMEM),
                  pl.BlockSpec((None, 2, tt), lambda b, j, *_: (jnp.minimum(b * nb + j + 1, nblk - 1), 0, 0),
                               memory_space=pltpu.SMEM),
                  pl.BlockSpec(memory_space=pl.ANY),
                  pl.BlockSpec((1, d), lambda b, j, *_: (0, 0))],
        out_specs=pl.BlockSpec((1, tt, d), lambda b, j, *_: (b, j, 0)),
        scratch_shapes=[pltpu.VMEM((2, tt * p, LANES), F32), pltpu.VMEM((2, tt * p, LANES), F32),
                        pltpu.SemaphoreType.DMA((2,))],
    )
    return pl.pallas_call(
        functools.partial(_combine_kernel, tt=tt, n_lat=n_lat, final=final),
        grid_spec=grid_spec,
        out_shape=jax.ShapeDtypeStruct((n_batch, out_rows, d), F32),
        compiler_params=_params(("arbitrary", "arbitrary")),
        name="moe_combine",
    )(xs, mod3, mod3, wts, slot_rows, slot_rows, y, final_g)


def _moe(xs, mod3, layer, g2, router_w, wg, wu, wd, i, final_g, n_lat, final):
    n_batch, t_all, d = xs.shape
    n_exp = wg.shape[1]
    tt = TQ
    tm = 512
    rows = n_lat if final else t_all
    nb = rows // tt
    n_assign = 2 * n_batch * nb * tt
    n_tiles = -(-n_assign // tm) + n_exp
    tr = _pick_tile(rows, 1024)
    assert tr % tt == 0
    info, wts, cnt = _route(xs, mod3, layer, g2, router_w, n_lat, rows // tr, tr)
    info = info.reshape(-1, 8, tr // tt, tt).transpose(0, 2, 1, 3).reshape(-1, 8, tt)
    counts = cnt[0, :n_exp].astype(jnp.int32)
    tiles_per = (counts + tm - 1) // tm
    tile_end = jnp.cumsum(tiles_per)
    starts = (tile_end - tiles_per) * tm
    n_used = tile_end[-1:]
    tile_ids = jnp.arange(n_tiles, dtype=jnp.int32)
    tile_expert = jnp.minimum(jnp.sum(tile_ids[:, None] >= tile_end[None, :], axis=1), n_exp - 1).astype(jnp.int32)
    expert, rank = info[:, 0:2, :], info[:, 2:4, :]
    group_start = jnp.sum(jnp.where(expert[..., None] == jnp.arange(n_exp, dtype=jnp.int32), starts, 0), axis=-1)
    slot_rows = (group_start + rank) * (d // LANES)
    xsort = _dispatch(xs, mod3, layer, g2, slot_rows, starts + counts, n_used, n_lat, nb, tt, tm,
                      (n_tiles + 1) * tm)
    y = _experts(xsort, tile_expert, n_used, wg, wu, wd, i, tm, n_tiles)
    return _combine(xs, mod3, layer, wts, slot_rows, y, final_g, n_lat, nb, tt, final)


def _final_kernel(x_ref, g_ref, o_ref):
    x = x_ref[0]
    o_ref[0] = x * lax.rsqrt(jnp.mean(x * x, axis=-1, keepdims=True) + NORM_EPS) * g_ref[...]


def _final_norm(xs, g, n_lat):
    n_batch, _, d = xs.shape
    tt = _pick_tile(n_lat, 256)
    return pl.pallas_call(
        _final_kernel,
        grid=(n_batch, n_lat // tt),
        in_specs=[pl.BlockSpec((1, tt, d), lambda b, t: (b, t, 0)), pl.BlockSpec((1, d), lambda b, t: (0, 0))],
        out_specs=pl.BlockSpec((1, tt, d), lambda b, t: (b, t, 0)),
        out_shape=jax.ShapeDtypeStruct((n_batch, n_lat, d), F32),
        compiler_params=_params(("arbitrary", "arbitrary")),
        name="final_norm",
    )(xs, g)


def kernel(x, c, ctx, c_ctx, ada_w, ada_b, norm1_g, norm2_g, ev_w_in, ev_conv_w, ev_ln_g, ev_ln_b,
           ev_q_norm_g, ev_k_norm_g, ev_w_out, ev_ffn_wg, ev_ffn_wu, ev_ffn_wd, od_w_in, od_lam,
           od_subln_g, od_w_out, od_router_w, od_moe_wg, od_moe_wu, od_moe_wd, final_norm_g):
    n_batch, n_lat, d = x.shape
    n_ctx = ctx.shape[1]
    depth = ada_w.shape[0]
    assert n_batch + 1 <= MOD_ROWS and n_lat % n_ctx == 0 and n_ctx % TQ == 0 and d % LANES == 0

    cond = jnp.zeros((MOD_ROWS, d), F32).at[:n_batch].set(c).at[n_batch].set(c_ctx)
    mod3 = _adaln(cond, ada_w, ada_b).reshape(depth * MOD_ROWS, 1, 6 * d)
    tabs = _rope_tables(n_lat, n_ctx)
    hs = _head_sum_matrix()
    xs = jnp.concatenate([x, ctx], axis=1)
    ffn_w = [w.astype(BF16) for w in (ev_ffn_wg, ev_ffn_wu, ev_ffn_wd)]
    moe_w = [w.astype(BF16) for w in (od_moe_wg, od_moe_wu, od_moe_wd)]

    for layer in range(depth):
        i = layer // 2
        g1 = norm1_g[layer].reshape(1, d)
        g2 = norm2_g[layer].reshape(1, d)
        if layer % 2 == 0:
            rep = LANES // HEAD_DIM
            qg = (jnp.tile(ev_q_norm_g[i], rep) * ATT_SCALE).reshape(1, LANES)
            kg = jnp.tile(ev_k_norm_g[i], rep).reshape(1, LANES)
            ag, q, k, v = _even_in(xs, mod3, layer, g1, ev_w_in[i].astype(BF16), qg, kg, tabs, hs, n_lat)
            att = _gqa(q, k, v, n_lat, n_ctx)
            cv = _conv(ag, ev_conv_w[i], ev_ln_g[i].reshape(1, -1), ev_ln_b[i].reshape(1, -1), n_lat, n_ctx)
            xs = _out_proj(xs, mod3, layer, [cv, att], ev_w_out[i].astype(BF16), n_lat)
            xs = _ffn(xs, mod3, layer, g2, *ffn_w, i, n_lat)
        else:
            lam_init = 0.8 - 0.6 * math.exp(-0.3 * layer)
            q, k, v = _odd_in(xs, mod3, layer, g1, od_w_in[i].astype(BF16), tabs, n_lat)
            final = layer == depth - 1
            att = _diff_attn(q, k, v, od_lam[i], od_subln_g[i].reshape(1, -1), lam_init, n_lat, n_ctx,
                             need_ctx=not final)
            xs = _out_proj(xs, mod3, layer, [att], od_w_out[i].astype(BF16), n_lat)
            xs = _moe(xs, mod3, layer, g2, od_router_w[i], *moe_w, i, final_norm_g.reshape(1, d),
                      n_lat, final=final)
    if depth % 2 == 0:
        return xs
    return _final_norm(xs, final_norm_g.reshape(1, d), n_lat)
```

```python
import functools
import math

import jax
import jax.numpy as jnp
from jax import lax
from jax.experimental import pallas as pl
from jax.experimental.pallas import tpu as pltpu

F32 = jnp.float32
BF16 = jnp.bfloat16

HEAD_DIM = 64
GRID_W = 64
ROPE_BASE = 10000.0
NORM_EPS = 1e-6
LN_EPS = 1e-5
CONV_WIDTH = 31
N_EXPERTS = 8
LANES = 128
MOD_ROWS = 24
VMEM_LIMIT = 56 * 1024 * 1024
ATT_SCALE = HEAD_DIM ** -0.5 * math.log2(math.e)
TQ = 256
FFN_TILE = 512
CONV_CHUNK = 64


def _params(sem):
    return pltpu.CompilerParams(dimension_semantics=sem, vmem_limit_bytes=VMEM_LIMIT)


def _pick_tile(total, target):
    best = 8
    for cand in range(8, min(total, target) + 1, 8):
        if total % cand == 0:
            best = cand
    return best


def _adaln_kernel(s_ref, w_ref, b_ref, o_ref):
    s = s_ref[...]
    s = s * jax.nn.sigmoid(s)
    o_ref[0] = jnp.dot(s, w_ref[0], precision=lax.Precision.HIGHEST,
                       preferred_element_type=F32) + b_ref[0]


def _adaln(cond, ada_w, ada_b):
    depth, d, six_d = ada_w.shape
    tn = six_d // 4
    return pl.pallas_call(
        _adaln_kernel,
        grid=(depth, six_d // tn),
        in_specs=[pl.BlockSpec((MOD_ROWS, d), lambda l, n: (0, 0)),
                  pl.BlockSpec((1, d, tn), lambda l, n: (l, 0, n)),
                  pl.BlockSpec((1, 1, tn), lambda l, n: (l, 0, n))],
        out_specs=pl.BlockSpec((1, MOD_ROWS, tn), lambda l, n: (l, 0, n)),
        out_shape=jax.ShapeDtypeStruct((depth, MOD_ROWS, six_d), F32),
        compiler_params=_params(("arbitrary", "arbitrary")),
        name="adaln",
    )(cond, ada_w, ada_b.reshape(depth, 1, six_d))


def _is_ctx_rows(t, tt, n_lat):
    rows = t * tt + lax.broadcasted_iota(jnp.int32, (tt, 1), 0)
    return rows >= n_lat


def _mod_pick(is_ctx, modl_ref, modc_ref, idx, d):
    return jnp.where(is_ctx, modc_ref[:, idx * d:(idx + 1) * d], modl_ref[:, idx * d:(idx + 1) * d])


def _norm_mod(x, g, is_ctx, modl_ref, modc_ref, shift_idx):
    d = x.shape[-1]
    r = lax.rsqrt(jnp.mean(x * x, axis=-1, keepdims=True) + NORM_EPS)
    shift = _mod_pick(is_ctx, modl_ref, modc_ref, shift_idx, d)
    scale = _mod_pick(is_ctx, modl_ref, modc_ref, shift_idx + 1, d)
    return (x * r * g) * (1.0 + scale) + shift


def _rope(xs, cos, sin_a, sin_b):
    return xs * cos + pltpu.roll(xs, LANES - 16, 1) * sin_a + pltpu.roll(xs, 16, 1) * sin_b


def _rope_tables(n_lat, n_ctx):
    rows = n_lat // GRID_W
    r = jnp.broadcast_to(jnp.arange(rows, dtype=F32)[:, None], (rows, GRID_W)).reshape(-1)
    col = jnp.broadcast_to(jnp.arange(GRID_W, dtype=F32)[None, :], (rows, GRID_W)).reshape(-1)
    quarter = HEAD_DIM // 4
    inv_freq = ROPE_BASE ** (-jnp.arange(quarter, dtype=F32) / quarter)
    ang_r = r[:, None] * inv_freq
    ang_c = col[:, None] * inv_freq
    zeros = jnp.zeros_like(ang_r)
    cos64 = jnp.concatenate([jnp.cos(ang_r), jnp.cos(ang_r), jnp.cos(ang_c), jnp.cos(ang_c)], axis=1)
    sin_a64 = jnp.concatenate([-jnp.sin(ang_r), zeros, -jnp.sin(ang_c), zeros], axis=1)
    sin_b64 = jnp.concatenate([zeros, jnp.sin(ang_r), zeros, jnp.sin(ang_c)], axis=1)

    def full(tab, ctx_val):
        tab = jnp.concatenate([tab, jnp.full((n_ctx, HEAD_DIM), ctx_val, F32)], axis=0)
        return jnp.tile(tab, (1, LANES // HEAD_DIM))

    return full(cos64, 1.0), full(sin_a64, 0.0), full(sin_b64, 0.0)


def _head_sum_matrix():
    i = jnp.arange(LANES) // HEAD_DIM
    return (i[:, None] == i[None, :]).astype(BF16)


def _mod_specs(layer, n_batch, six_d, nargs):
    if nargs == 2:
        lat = lambda b, t: (layer * MOD_ROWS + b, 0, 0)
        ctx = lambda b, t: (layer * MOD_ROWS + n_batch, 0, 0)
    elif nargs == 3:
        lat = lambda b, t, f: (layer * MOD_ROWS + b, 0, 0)
        ctx = lambda b, t, f: (layer * MOD_ROWS + n_batch, 0, 0)
    else:
        lat = lambda b, t, e, f: (layer * MOD_ROWS + b, 0, 0)
        ctx = lambda b, t, e, f: (layer * MOD_ROWS + n_batch, 0, 0)
    return [pl.BlockSpec((None, 1, six_d), lat), pl.BlockSpec((None, 1, six_d), ctx)]


def _even_in_kernel(x_ref, modl_ref, modc_ref, g_ref, w_ref, qg_ref, kg_ref, cos_ref, sa_ref, sb_ref,
                    hs_ref, ag_ref, q_ref, k_ref, v_ref, *, tt, n_lat):
    d = x_ref.shape[-1]
    conv2 = d
    qw = d // 2
    kvw = (w_ref.shape[1] - conv2 - qw) // 2
    is_ctx = _is_ctx_rows(pl.program_id(1), tt, n_lat)
    h = _norm_mod(x_ref[0], g_ref[...], is_ctx, modl_ref, modc_ref, 0).astype(BF16)
    half = conv2 // 2
    for c in range(2):
        ag_ref[0, :, c * half:(c + 1) * half] = jnp.dot(
            h, w_ref[:, c * half:(c + 1) * half], preferred_element_type=F32)
    cos, sa, sb = cos_ref[...], sa_ref[...], sb_ref[...]
    hs = hs_ref[...]

    def head_norm_rope(ys, gain):
        ss = jnp.dot((ys * ys).astype(BF16), hs, preferred_element_type=F32)
        yn = ys * lax.rsqrt(ss * (1.0 / HEAD_DIM) + NORM_EPS) * gain
        return _rope(yn, cos, sa, sb).astype(BF16)

    q = jnp.dot(h, w_ref[:, conv2:conv2 + qw], preferred_element_type=F32)
    for c in range(qw // LANES):
        q_ref[0, :, c * LANES:(c + 1) * LANES] = head_norm_rope(q[:, c * LANES:(c + 1) * LANES], qg_ref[...])
    kv = jnp.dot(h, w_ref[:, conv2 + qw:], preferred_element_type=F32)
    for c in range(kvw // LANES):
        k_ref[0, :, c * LANES:(c + 1) * LANES] = head_norm_rope(kv[:, c * LANES:(c + 1) * LANES], kg_ref[...])
    v_ref[0] = kv[:, kvw:].astype(BF16)


def _even_in(xs, mod3, layer, g1, w_in, qg, kg, tabs, hs, n_lat):
    n_batch, t_all, d = xs.shape
    tt = _pick_tile(t_all, 768)
    qw = d // 2
    kvw = (w_in.shape[1] - d - qw) // 2
    full2 = lambda b, t: (0, 0)
    tok = lambda b, t: (b, t, 0)
    tab_spec = pl.BlockSpec((tt, LANES), lambda b, t: (t, 0))
    return pl.pallas_call(
        functools.partial(_even_in_kernel, tt=tt, n_lat=n_lat),
        grid=(n_batch, t_all // tt),
        in_specs=[pl.BlockSpec((1, tt, d), tok)] + _mod_specs(layer, n_batch, 6 * d, 2) + [
            pl.BlockSpec((1, d), full2),
            pl.BlockSpec(w_in.shape, full2),
            pl.BlockSpec((1, LANES), full2), pl.BlockSpec((1, LANES), full2),
            tab_spec, tab_spec, tab_spec,
            pl.BlockSpec((LANES, LANES), full2)],
        out_specs=[pl.BlockSpec((1, tt, d), tok), pl.BlockSpec((1, tt, qw), tok),
                   pl.BlockSpec((1, tt, kvw), tok), pl.BlockSpec((1, tt, kvw), tok)],
        out_shape=[jax.ShapeDtypeStruct((n_batch, t_all, d), F32),
                   jax.ShapeDtypeStruct((n_batch, t_all, qw), BF16),
                   jax.ShapeDtypeStruct((n_batch, t_all, kvw), BF16),
                   jax.ShapeDtypeStruct((n_batch, t_all, kvw), BF16)],
        compiler_params=_params(("arbitrary", "arbitrary")),
        name="even_in",
    )(xs, mod3, mod3, g1, w_in, qg, kg, *tabs, hs)


def _pipe_item(n, n_items, lag, n_mid, nq):
    j = jnp.clip(n - lag, 0, n_items - 1)
    return j // (n_mid * nq), (j // nq) % n_mid, j % nq


def _gqa_replicate(src_ref, dst_ref, kv):
    a = src_ref[0].astype(F32)
    lane = lax.broadcasted_iota(jnp.int32, a.shape, 1)
    keep = jnp.logical_xor(lane < HEAD_DIM, kv == 1)
    rep = jnp.where(keep, a, pltpu.roll(a, HEAD_DIM, 1)).astype(BF16)
    for c in range(dst_ref.shape[-1] // LANES):
        dst_ref[:, c * LANES:(c + 1) * LANES] = rep


def _gqa_stack(q):
    head = lax.broadcasted_iota(jnp.int32, q.shape, 1) // HEAD_DIM
    group = q.shape[-1] // HEAD_DIM
    return jnp.concatenate([jnp.where(head == g, q, jnp.zeros_like(q)) for g in range(group)], axis=0)


def _gqa_unstack(o, rows):
    group = o.shape[-1] // HEAD_DIM
    head = lax.broadcasted_iota(jnp.int32, (rows, o.shape[-1]), 1) // HEAD_DIM
    out = jnp.zeros((rows, o.shape[-1]), F32)
    for g in range(group):
        out = out + jnp.where(head == g, o[g * rows:(g + 1) * rows], 0.0)
    return out.astype(BF16)


def _gqa_lat_kernel(q_ref, k_ref, v_ref, o_ref, k4_ref, v4_ref, s0, s1, p0, p1, l0, l1, *, n_items, nq, n_kv):
    n = pl.program_id(0)
    _, kv_qk, qi_qk = _pipe_item(n, n_items, 0, n_kv, nq)
    _, kv_pv, qi_pv = _pipe_item(n, n_items, 2, n_kv, nq)

    @pl.when(n == 0)
    def _():
        s1[...] = jnp.zeros_like(s1)
        p0[...] = jnp.zeros_like(p0)
        l0[...] = jnp.ones_like(l0)

    @pl.when((n < n_items) & (qi_qk == 0))
    def _():
        _gqa_replicate(k_ref, k4_ref, kv_qk)

    @pl.when((n == 0) | ((n >= 2) & (qi_pv == 0)))
    def _():
        _gqa_replicate(v_ref, v4_ref, kv_pv)

    def stages(s_w, s_r, p_w, l_w, p_r, l_r):
        s_w[...] = lax.dot_general(_gqa_stack(q_ref[0]), k4_ref[...], (((1,), (1,)), ((), ())),
                                   preferred_element_type=F32)
        s = s_r[...]
        p = jnp.exp2(s - jnp.max(s, axis=-1, keepdims=True))
        l_w[...] = jnp.sum(p, axis=-1, keepdims=True)
        p_w[...] = p.astype(BF16)
        o = jnp.dot(p_r[...], v4_ref[...], preferred_element_type=F32) / l_r[...]
        o_ref[0] = _gqa_unstack(o, o_ref.shape[1])

    @pl.when(n % 2 == 0)
    def _():
        stages(s0, s1, p1, l1, p0, l0)

    @pl.when(n % 2 == 1)
    def _():
        stages(s1, s0, p0, l0, p1, l1)


def _gqa_ctx_kernel(q_ref, k_ref, v_ref, o_ref, k4_ref, v4_ref):
    _gqa_replicate(k_ref, k4_ref, pl.program_id(1))
    _gqa_replicate(v_ref, v4_ref, pl.program_id(1))
    s = lax.dot_general(_gqa_stack(q_ref[0]), k4_ref[...], (((1,), (1,)), ((), ())),
                        preferred_element_type=F32)
    p = jnp.exp2(s - jnp.max(s, axis=-1, keepdims=True))
    l = jnp.sum(p, axis=-1, keepdims=True)
    o = jnp.dot(p.astype(BF16), v4_ref[...], preferred_element_type=F32) / l
    o_ref[0] = _gqa_unstack(o, o_ref.shape[1])


def _gqa(q, k, v, n_lat, n_ctx):
    n_batch, t_all, qw = q.shape
    kvw = k.shape[-1]
    n_kv = kvw // HEAD_DIM
    gw = qw // n_kv
    group = gw // HEAD_DIM
    nq = n_lat // TQ
    n_items = n_batch * n_kv * nq
    qk = lambda n: _pipe_item(n, n_items, 0, n_kv, nq)
    pv = lambda n: _pipe_item(n, n_items, 2, n_kv, nq)
    att = pl.pallas_call(
        functools.partial(_gqa_lat_kernel, n_items=n_items, nq=nq, n_kv=n_kv),
        grid=(n_items + 2,),
        in_specs=[pl.BlockSpec((1, TQ, gw), lambda n: (qk(n)[0], qk(n)[2], qk(n)[1])),
                  pl.BlockSpec((1, t_all, kvw), lambda n: (qk(n)[0], 0, 0)),
                  pl.BlockSpec((1, t_all, kvw), lambda n: (pv(n)[0], 0, 0))],
        out_specs=pl.BlockSpec((1, TQ, gw), lambda n: (pv(n)[0], pv(n)[2], pv(n)[1])),
        out_shape=jax.ShapeDtypeStruct((n_batch, n_lat, qw), BF16),
        scratch_shapes=[pltpu.VMEM((t_all, gw), BF16), pltpu.VMEM((t_all, gw), BF16),
                        pltpu.VMEM((group * TQ, t_all), F32), pltpu.VMEM((group * TQ, t_all), F32),
                        pltpu.VMEM((group * TQ, t_all), BF16), pltpu.VMEM((group * TQ, t_all), BF16),
                        pltpu.VMEM((group * TQ, 1), F32), pltpu.VMEM((group * TQ, 1), F32)],
        compiler_params=_params(("arbitrary",)),
        name="gqa_lat",
    )(q, k, v)
    cblk = n_lat // n_ctx
    att_ctx = pl.pallas_call(
        _gqa_ctx_kernel,
        grid=(n_batch, n_kv, n_ctx // TQ),
        in_specs=[pl.BlockSpec((1, TQ, gw), lambda b, h, i: (b, nq + i, h)),
                  pl.BlockSpec((1, n_ctx, kvw), lambda b, h, i: (b, cblk, 0)),
                  pl.BlockSpec((1, n_ctx, kvw), lambda b, h, i: (b, cblk, 0))],
        out_specs=pl.BlockSpec((1, TQ, gw), lambda b, h, i: (b, i, h)),
        out_shape=jax.ShapeDtypeStruct((n_batch, n_ctx, qw), BF16),
        scratch_shapes=[pltpu.VMEM((n_ctx, gw), BF16), pltpu.VMEM((n_ctx, gw), BF16)],
        compiler_params=_params(("arbitrary", "arbitrary", "arbitrary")),
        name="gqa_ctx",
    )(q, k, v)
    return jnp.concatenate([att, att_ctx], axis=1)


def _conv_kernel(ag_ref, w_ref, lng_ref, lnb_ref, o_ref, ypad_ref, *, n_lat, n_ctx):
    ch = o_ref.shape[-1]
    pad = CONV_WIDTH // 2
    halo = 16
    lat0 = halo
    ctx0 = 2 * halo + n_lat
    zeros = jnp.zeros((halo, ch), F32)
    ypad_ref[0:halo, :] = zeros
    ypad_ref[halo + n_lat:ctx0, :] = zeros
    ypad_ref[ctx0 + n_ctx:ctx0 + n_ctx + halo, :] = zeros

    def glu(src0, dst0, n):
        def body(i, carry):
            r = pl.multiple_of(i * CONV_CHUNK, CONV_CHUNK)
            a = ag_ref[0, pl.ds(src0 + r, CONV_CHUNK), 0:ch]
            g = ag_ref[0, pl.ds(src0 + r, CONV_CHUNK), ch:2 * ch]
            ypad_ref[pl.ds(dst0 + r, CONV_CHUNK), :] = a * jax.nn.sigmoid(g)
            return carry
        lax.fori_loop(0, n // CONV_CHUNK, body, 0)

    glu(0, lat0, n_lat)
    glu(n_lat, ctx0, n_ctx)

    def conv(src0, dst0, n):
        def body(i, carry):
            r = pl.multiple_of(i * CONV_CHUNK, CONV_CHUNK)
            accs = []
            for c in range(ch // LANES):
                cols = slice(c * LANES, (c + 1) * LANES)
                win = ypad_ref[pl.ds(src0 - halo + r, CONV_CHUNK + 2 * halo), cols]
                acc = jnp.zeros((CONV_CHUNK, LANES), F32)
                for s in range(8):
                    taps = [j for j in range(CONV_WIDTH) if (j + halo - pad) % 8 == s]
                    if not taps:
                        continue
                    sh = win[s:s + CONV_CHUNK + 2 * halo - 8]
                    for j in taps:
                        a8 = (j + halo - pad) - s
                        acc = acc + w_ref[j:j + 1, cols] * sh[a8:a8 + CONV_CHUNK]
                accs.append(acc)
            mu = sum(jnp.sum(a, axis=-1, keepdims=True) for a in accs) * (1.0 / ch)
            cens = [a - mu for a in accs]
            var = sum(jnp.sum(cn * cn, axis=-1, keepdims=True) for cn in cens) * (1.0 / ch)
            rstd = lax.rsqrt(var + LN_EPS)
            for c, cn in enumerate(cens):
                cols = slice(c * LANES, (c + 1) * LANES)
                yn = cn * rstd * lng_ref[:, cols] + lnb_ref[:, cols]
                o_ref[0, pl.ds(dst0 + r, CONV_CHUNK), cols] = (yn * jax.nn.sigmoid(yn)).astype(BF16)
            return carry
        lax.fori_loop(0, n // CONV_CHUNK, body, 0)

    conv(lat0, 0, n_lat)
    conv(ctx0, n_lat, n_ctx)


def _conv(ag, conv_w, ln_g, ln_b, n_lat, n_ctx):
    n_batch, t_all, two_ch = ag.shape
    ch = two_ch // 2
    full2 = lambda b: (0, 0)
    return pl.pallas_call(
        functools.partial(_conv_kernel, n_lat=n_lat, n_ctx=n_ctx),
        grid=(n_batch,),
        in_specs=[pl.BlockSpec((1, t_all, two_ch), lambda b: (b, 0, 0)),
                  pl.BlockSpec(conv_w.shape, full2),
                  pl.BlockSpec((1, ch), full2), pl.BlockSpec((1, ch), full2)],
        out_specs=pl.BlockSpec((1, t_all, ch), lambda b: (b, 0, 0)),
        out_shape=jax.ShapeDtypeStruct((n_batch, t_all, ch), BF16),
        scratch_shapes=[pltpu.VMEM((t_all + 48, ch), F32)],
        compiler_params=_params(("arbitrary",)),
        name="conv",
    )(ag, conv_w, ln_g, ln_b)


def _odd_in_kernel(x_ref, modl_ref, modc_ref, g_ref, w_ref, cos_ref, sa_ref, sb_ref,
                   q_ref, k_ref, v_ref, *, tt, n_lat):
    d = x_ref.shape[-1]
    is_ctx = _is_ctx_rows(pl.program_id(1), tt, n_lat)
    h = _norm_mod(x_ref[0], g_ref[...], is_ctx, modl_ref, modc_ref, 0).astype(BF16)
    cos, sa, sb = cos_ref[...], sa_ref[...], sb_ref[...]
    cw = 512
    for part, (dst, scale) in enumerate(((q_ref, ATT_SCALE), (k_ref, None), (v_ref, None))):
        for c0 in range(0, d, cw):
            y = jnp.dot(h, w_ref[:, part * d + c0:part * d + c0 + cw], preferred_element_type=F32)
            if part == 2:
                dst[0, :, c0:c0 + cw] = y.astype(BF16)
                continue
            for c in range(cw // LANES):
                ys = _rope(y[:, c * LANES:(c + 1) * LANES], cos, sa, sb)
                if scale is not None:
                    ys = ys * scale
                dst[0, :, c0 + c * LANES:c0 + (c + 1) * LANES] = ys.astype(BF16)


def _odd_in(xs, mod3, layer, g1, w_in, tabs, n_lat):
    n_batch, t_all, d = xs.shape
    tt = _pick_tile(t_all, 768)
    full2 = lambda b, t: (0, 0)
    tok = lambda b, t: (b, t, 0)
    tab_spec = pl.BlockSpec((tt, LANES), lambda b, t: (t, 0))
    return pl.pallas_call(
        functools.partial(_odd_in_kernel, tt=tt, n_lat=n_lat),
        grid=(n_batch, t_all // tt),
        in_specs=[pl.BlockSpec((1, tt, d), tok)] + _mod_specs(layer, n_batch, 6 * d, 2) + [
            pl.BlockSpec((1, d), full2),
            pl.BlockSpec(w_in.shape, full2),
            tab_spec, tab_spec, tab_spec],
        out_specs=[pl.BlockSpec((1, tt, d), tok)] * 3,
        out_shape=[jax.ShapeDtypeStruct((n_batch, t_all, d), BF16)] * 3,
        compiler_params=_params(("arbitrary", "arbitrary")),
        name="odd_in",
    )(xs, mod3, mod3, g1, w_in, *tabs)


def _diff_lambda(lam_ref, lam_init):
    lp = lam_ref[...]
    return (jnp.exp(jnp.sum(lp[0:1] * lp[1:2], axis=-1, keepdims=True))
            - jnp.exp(jnp.sum(lp[2:3] * lp[3:4], axis=-1, keepdims=True)) + lam_init)


def _diff_scores(q, k):
    first = lax.broadcasted_iota(jnp.int32, q.shape, 1) < HEAD_DIM
    zero = jnp.zeros_like(q)
    qs = jnp.concatenate([jnp.where(first, q, zero), jnp.where(first, zero, q)], axis=0)
    return lax.dot_general(qs, k, (((1,), (1,)), ((), ())), preferred_element_type=F32)


def _diff_weights(s, lam):
    rows = s.shape[0] // 2
    p = jnp.exp2(s - jnp.max(s, axis=-1, keepdims=True))
    p = p / jnp.sum(p, axis=-1, keepdims=True)
    return (p[0:rows] - lam * p[rows:]).astype(BF16)


def _diff_finish(o, sg_ref, lam_init):
    r = lax.rsqrt(jnp.mean(o * o, axis=-1, keepdims=True) + NORM_EPS)
    return ((o * r * sg_ref[...]) * (1.0 - lam_init)).astype(BF16)


def _diff_lat_kernel(q_ref, k_ref, v_ref, lam_ref, sg_ref, o_ref, va_ref, s0, s1, a0, a1,
                     *, lam_init, n_items, nq, n_heads):
    n = pl.program_id(0)
    lam = _diff_lambda(lam_ref, lam_init)
    _, _, qi_pv = _pipe_item(n, n_items, 2, n_heads, nq)

    @pl.when(n == 0)
    def _():
        s1[...] = jnp.zeros_like(s1)
        a0[...] = jnp.ones_like(a0)
        va_ref[:, v_ref.shape[-1]:] = jnp.ones((va_ref.shape[0], va_ref.shape[1] - v_ref.shape[-1]), BF16)

    @pl.when((n == 0) | ((n >= 2) & (qi_pv == 0)))
    def _():
        va_ref[:, 0:v_ref.shape[-1]] = v_ref[0]

    def stages(s_w, s_r, a_w, a_r):
        rows = q_ref.shape[1]
        vw = v_ref.shape[-1]
        s_w[...] = _diff_scores(q_ref[0], k_ref[0])
        s = s_r[...]
        a_w[...] = jnp.exp2(s - jnp.max(s, axis=-1, keepdims=True)).astype(BF16)
        oa = jnp.dot(a_r[...], va_ref[...], preferred_element_type=F32)
        o = oa[0:rows, 0:vw] / oa[0:rows, vw:vw + 1] - lam * (oa[rows:, 0:vw] / oa[rows:, vw:vw + 1])
        o_ref[0] = _diff_finish(o, sg_ref, lam_init)

    @pl.when(n % 2 == 0)
    def _():
        stages(s0, s1, a1, a0)

    @pl.when(n % 2 == 1)
    def _():
        stages(s1, s0, a0, a1)


def _diff_ctx_kernel(q_ref, k_ref, v_ref, lam_ref, sg_ref, o_ref, *, lam_init):
    a =_diff_weights(_diff_scores(q_ref[0], k_ref[0]), _diff_lambda(lam_ref, lam_init))
    o_ref[0] = _diff_finish(jnp.dot(a, v_ref[0], preferred_element_type=F32), sg_ref, lam_init)


def _diff_attn(q, k, v, lam_p, subln_g, lam_init, n_lat, n_ctx, need_ctx):
    n_batch, t_all, d = q.shape
    hw = 2 * HEAD_DIM
    n_heads = d // hw
    tq = TQ
    nq = n_lat // tq
    n_items = n_batch * n_heads * nq
    qk = lambda n: _pipe_item(n, n_items, 0, n_heads, nq)
    pv = lambda n: _pipe_item(n, n_items, 2, n_heads, nq)
    att = pl.pallas_call(
        functools.partial(_diff_lat_kernel, lam_init=lam_init, n_items=n_items, nq=nq, n_heads=n_heads),
        grid=(n_items + 2,),
        in_specs=[pl.BlockSpec((1, tq, hw), lambda n: (qk(n)[0], qk(n)[2], qk(n)[1])),
                  pl.BlockSpec((1, t_all, hw), lambda n: (qk(n)[0], 0, qk(n)[1])),
                  pl.BlockSpec((1, t_all, hw), lambda n: (pv(n)[0], 0, pv(n)[1])),
                  pl.BlockSpec(lam_p.shape, lambda n: (0, 0)),
                  pl.BlockSpec((1, hw), lambda n: (0, 0))],
        out_specs=pl.BlockSpec((1, tq, hw), lambda n: (pv(n)[0], pv(n)[2], pv(n)[1])),
        out_shape=jax.ShapeDtypeStruct((n_batch, n_lat, d), BF16),
        scratch_shapes=[pltpu.VMEM((t_all, 2 * hw), BF16),
                        pltpu.VMEM((2 * tq, t_all), F32), pltpu.VMEM((2 * tq, t_all), F32),
                        pltpu.VMEM((2 * tq, t_all), BF16), pltpu.VMEM((2 * tq, t_all), BF16)],
        compiler_params=_params(("arbitrary",)),
        name="diff_lat",
    )(q, k, v, lam_p, subln_g)
    if not need_ctx:
        return jnp.concatenate([att, jnp.zeros((n_batch, n_ctx, d), BF16)], axis=1)
    cblk = n_lat // n_ctx
    full2 = lambda b, h, i: (0, 0)
    att_ctx = pl.pallas_call(
        functools.partial(_diff_ctx_kernel, lam_init=lam_init),
        grid=(n_batch, n_heads, n_ctx // TQ),
        in_specs=[pl.BlockSpec((1, TQ, hw), lambda b, h, i: (b, n_lat // TQ + i, h)),
                  pl.BlockSpec((1, n_ctx, hw), lambda b, h, i: (b, cblk, h)),
                  pl.BlockSpec((1, n_ctx, hw), lambda b, h, i: (b, cblk, h)),
                  pl.BlockSpec(lam_p.shape, full2),
                  pl.BlockSpec((1, hw), full2)],
        out_specs=pl.BlockSpec((1, TQ, hw), lambda b, h, i: (b, i, h)),
        out_shape=jax.ShapeDtypeStruct((n_batch, n_ctx, d), BF16),
        compiler_params=_params(("arbitrary", "arbitrary", "arbitrary")),
        name="diff_ctx",
    )(q, k, v, lam_p, subln_g)
    return jnp.concatenate([att, att_ctx], axis=1)


def _out_kernel(*refs, tt, n_lat, n_parts):
    x_ref, modl_ref, modc_ref = refs[0:3]
    parts = refs[3:3 + n_parts]
    w_ref = refs[3 + n_parts]
    o_ref = refs[4 + n_parts]
    d = x_ref.shape[-1]
    is_ctx = _is_ctx_rows(pl.program_id(1), tt, n_lat)
    gate = _mod_pick(is_ctx, modl_ref, modc_ref, 2, d)
    y = jnp.zeros((tt, d), F32)
    row = 0
    for p in parts:
        kp = p.shape[-1]
        y = y + jnp.dot(p[0], w_ref[row:row + kp, :], preferred_element_type=F32)
        row += kp
    o_ref[0] = x_ref[0] + gate * y


def _out_proj(xs, mod3, layer, parts, w_out, n_lat):
    n_batch, t_all, d = xs.shape
    tt = _pick_tile(t_all, 768)
    tok = lambda b, t: (b, t, 0)
    return pl.pallas_call(
        functools.partial(_out_kernel, tt=tt, n_lat=n_lat, n_parts=len(parts)),
        grid=(n_batch, t_all // tt),
        in_specs=[pl.BlockSpec((1, tt, d), tok)] + _mod_specs(layer, n_batch, 6 * d, 2)
        + [pl.BlockSpec((1, tt, p.shape[-1]), tok) for p in parts]
        + [pl.BlockSpec(w_out.shape, lambda b, t: (0, 0))],
        out_specs=pl.BlockSpec((1, tt, d), tok),
        out_shape=jax.ShapeDtypeStruct(xs.shape, F32),
        compiler_params=_params(("arbitrary", "arbitrary")),
        name="out_proj",
    )(xs, mod3, mod3, *parts, w_out)


def _ffn_kernel(x_ref, modl_ref, modc_ref, g_ref, cv_ref, att_ref, wo_ref, wg_ref, wu_ref, wd_ref, o_ref,
                x1_ref, h_ref, acc_ref, *, tt, n_lat):
    f = pl.program_id(2)
    d = x_ref.shape[-1]
    is_ctx = _is_ctx_rows(pl.program_id(1), tt, n_lat)

    @pl.when(f == 0)
    def _():
        kc = cv_ref.shape[-1]
        y = (jnp.dot(cv_ref[0], wo_ref[0:kc, :], preferred_element_type=F32)
             + jnp.dot(att_ref[0], wo_ref[kc:, :], preferred_element_type=F32))
        x1 = x_ref[0] + _mod_pick(is_ctx, modl_ref, modc_ref, 2, d) * y
        x1_ref[...] = x1
        h_ref[...] = _norm_mod(x1, g_ref[...], is_ctx, modl_ref, modc_ref, 3).astype(BF16)
        acc_ref[...] = jnp.zeros_like(acc_ref)

    h = h_ref[...]
    g = jnp.dot(h, wg_ref[...], preferred_element_type=F32)
    u = jnp.dot(h, wu_ref[...], preferred_element_type=F32)
    act = (g * jax.nn.sigmoid(g) * u).astype(BF16)
    acc_ref[...] += jnp.dot(act, wd_ref[...], preferred_element_type=F32)

    @pl.when(f == pl.num_programs(2) - 1)
    def _():
        gate = _mod_pick(is_ctx, modl_ref, modc_ref, 5, d)
        o_ref[0] = x1_ref[...] + gate * acc_ref[...]


def _ffn(xs, mod3, layer, g2, cv, att, w_out, wg, wu, wd, i, n_lat):
    n_batch, t_all, d = xs.shape
    ffn = wg.shape[-1]
    tt = _pick_tile(t_all, 768)
    tf = _pick_tile(ffn, FFN_TILE)
    tok = lambda b, t, f: (b, t, 0)
    return pl.pallas_call(
        functools.partial(_ffn_kernel, tt=tt, n_lat=n_lat),
        grid=(n_batch, t_all // tt, ffn // tf),
        in_specs=[pl.BlockSpec((1, tt, d), tok)] + _mod_specs(layer, n_batch, 6 * d, 3) + [
            pl.BlockSpec((1, d), lambda b, t, f: (0, 0)),
            pl.BlockSpec((1, tt, cv.shape[-1]), tok),
            pl.BlockSpec((1, tt, att.shape[-1]), tok),
            pl.BlockSpec(w_out.shape, lambda b, t, f: (0, 0)),
            pl.BlockSpec((None, d, tf), lambda b, t, f: (i, 0, f)),
            pl.BlockSpec((None, d, tf), lambda b, t, f: (i, 0, f)),
            pl.BlockSpec((None, tf, d), lambda b, t, f: (i, f, 0))],
        out_specs=pl.BlockSpec((1, tt, d), tok),
        out_shape=jax.ShapeDtypeStruct(xs.shape, F32),
        scratch_shapes=[pltpu.VMEM((tt, d), F32), pltpu.VMEM((tt, d), BF16), pltpu.VMEM((tt, d), F32)],
        compiler_params=_params(("arbitrary", "arbitrary", "arbitrary")),
        name="ffn",
    )(xs, mod3, mod3, g2, cv, att, w_out, wg, wu, wd)


def _top2(logits):
    lane = lax.broadcasted_iota(jnp.int32, logits.shape, 1).astype(F32)
    neg = jnp.float32(-jnp.inf)
    lg = jnp.where(lane < N_EXPERTS, logits, neg)
    m1 = jnp.max(lg, axis=-1, keepdims=True)
    i1 = jnp.min(jnp.where(lg == m1, lane, float(LANES)), axis=-1, keepdims=True)
    lg2 = jnp.where(lane == i1, neg, lg)
    m2 = jnp.max(lg2, axis=-1, keepdims=True)
    i2 = jnp.min(jnp.where(lg2 == m2, lane, float(LANES)), axis=-1, keepdims=True)
    e2 = jnp.exp(m2 - m1)
    den = 1.0 + e2
    return lane, i1, i2, 1.0 / den, e2 / den


def _route_kernel(x_ref, modl_ref, modc_ref, g_ref, rw_ref, info_ref, w_ref, cnt_ref, run_ref, *, tt, n_lat):
    @pl.when((pl.program_id(0) == 0) & (pl.program_id(1) == 0))
    def _():
        run_ref[...] = jnp.zeros_like(run_ref)

    is_ctx = _is_ctx_rows(pl.program_id(1), tt, n_lat)
    h = _norm_mod(x_ref[0], g_ref[...], is_ctx, modl_ref, modc_ref, 3)
    logits = jnp.dot(h, rw_ref[...], precision=lax.Precision.HIGHEST, preferred_element_type=F32)
    lane, i1, i2, w1, w2 = _top2(logits)
    chosen = jnp.where((lane == i1) | (lane == i2), 1.0, 0.0)
    before = (lax.broadcasted_iota(jnp.int32, (tt, tt), 1) < lax.broadcasted_iota(jnp.int32, (tt, tt), 0))
    pre = jnp.dot(jnp.where(before, 1.0, 0.0).astype(BF16), chosen.astype(BF16),
                  preferred_element_type=F32) + run_ref[...]
    r1 = jnp.sum(jnp.where(lane == i1, pre, 0.0), axis=-1, keepdims=True)
    r2 = jnp.sum(jnp.where(lane == i2, pre, 0.0), axis=-1, keepdims=True)
    run_ref[...] += jnp.sum(chosen, axis=0, keepdims=True)
    cnt_ref[...] = run_ref[...]
    info = jnp.where(lane == 0.0, i1, jnp.where(lane == 1.0, i2, jnp.where(lane == 2.0, r1,
                                                                          jnp.where(lane == 3.0, r2, 0.0))))
    info_ref[0] = info.T[0:8].astype(jnp.int32)
    w_ref[0] = jnp.where(lane == 0.0, w1, jnp.where(lane == 1.0, w2, 0.0))


def _route(xs, mod3, layer, g2, router_w, n_lat, nb, tt):
    n_batch, t_all, d = xs.shape
    rw = jnp.zeros((d, LANES), F32).at[:, :router_w.shape[1]].set(router_w)
    return pl.pallas_call(
        functools.partial(_route_kernel, tt=tt, n_lat=n_lat),
        grid=(n_batch, nb),
        in_specs=[pl.BlockSpec((1, tt, d), lambda b, j: (b, j, 0))] + _mod_specs(layer, n_batch, 6 * d, 2) + [
            pl.BlockSpec((1, d), lambda b, j: (0, 0)),
            pl.BlockSpec((d, LANES), lambda b, j: (0, 0))],
        out_specs=[pl.BlockSpec((1, 8, tt), lambda b, j: (b * nb + j, 0, 0)),
                   pl.BlockSpec((1, tt, LANES), lambda b, j: (b, j, 0)),
                   pl.BlockSpec((1, LANES), lambda b, j: (0, 0))],
        out_shape=[jax.ShapeDtypeStruct((n_batch * nb, 8, tt), jnp.int32),
                   jax.ShapeDtypeStruct((n_batch, nb * tt, LANES), F32),
                   jax.ShapeDtypeStruct((1, LANES), F32)],
        scratch_shapes=[pltpu.VMEM((1, LANES), F32)],
        compiler_params=_params(("arbitrary", "arbitrary")),
        name="moe_route",
    )(xs, mod3, mod3, g2, rw)


def _slot_row(rows_ref, k, r, p):
    return pl.multiple_of(rows_ref[k, r], p)


def _dispatch_kernel(fill_ref, nu_ref, x_ref, modl_ref, modc_ref, g_ref, rows_ref, xs_ref,
                     hbuf_ref, zero_ref, sems, zsem, *, tt, n_lat, tm):
    nb = pl.num_programs(1)
    step = pl.program_id(0) * nb + pl.program_id(1)
    nsteps = pl.num_programs(0) * nb
    slot = lax.rem(step, 2)
    p = x_ref.shape[-1] // LANES

    def fill_copy(e):
        return pltpu.make_async_copy(zero_ref, xs_ref.at[pl.ds(pl.multiple_of(fill_ref[e] * p, p), tm * p)], zsem)

    @pl.when(step == 0)
    def _():
        zero_ref[...] = jnp.zeros_like(zero_ref)
        for e in range(N_EXPERTS):
            fill_copy(e).start()
        for e in range(N_EXPERTS):
            fill_copy(e).wait()

        def clear_tile(t, carry):
            cp = pltpu.make_async_copy(
                zero_ref, xs_ref.at[pl.ds(pl.multiple_of(t * (tm * p), tm * p), tm * p)], zsem)
            cp.start()
            cp.wait()
            return carry

        lax.fori_loop(nu_ref[0], xs_ref.shape[0] // (tm * p), clear_tile, 0)

    is_ctx = _is_ctx_rows(pl.program_id(1), tt, n_lat)
    h = _norm_mod(x_ref[0], g_ref[...], is_ctx, modl_ref, modc_ref, 3)
    for s in range(p):
        hbuf_ref[slot, pl.ds(s, tt, stride=p), :] = h[:, s * LANES:(s + 1) * LANES]

    def issue(r, carry):
        src = hbuf_ref.at[slot, pl.ds(pl.multiple_of(r * p, p), p)]
        for k in range(2):
            pltpu.make_async_copy(src, xs_ref.at[pl.ds(_slot_row(rows_ref, k, r, p), p)],
                                  sems.at[slot]).start()
        return carry

    lax.fori_loop(0, tt, issue, 0, unroll=8)

    def wait_rows(sl):
        for _ in range(2):
            pltpu.make_async_copy(hbuf_ref.at[sl], xs_ref.at[pl.ds(0, tt * p)], sems.at[sl]).wait()

    @pl.when(step > 0)
    def _():
        wait_rows(1 - slot)

    @pl.when(step == nsteps - 1)
    def _():
        wait_rows(slot)


def _dispatch(xs, mod3, layer, g2, slot_rows, fill, n_used, n_lat, nb, tt, tm, n_rows):
    n_batch, _, d = xs.shape
    p = d // LANES
    grid_spec = pltpu.PrefetchScalarGridSpec(
        num_scalar_prefetch=2,
        grid=(n_batch, nb),
        in_specs=[pl.BlockSpec((1, tt, d), lambda b, j, *_: (b, j, 0)),
                  pl.BlockSpec((None, 1, 6 * d), lambda b, j, *_: (layer * MOD_ROWS + b, 0, 0)),
                  pl.BlockSpec((None, 1, 6 * d), lambda b, j, *_: (layer * MOD_ROWS + n_batch, 0, 0)),
                  pl.BlockSpec((1, d), lambda b, j, *_: (0, 0)),
                  pl.BlockSpec((None, 2, tt), lambda b, j, *_: (b * nb + j, 0, 0), memory_space=pltpu.SMEM)],
        out_specs=pl.BlockSpec(memory_space=pl.ANY),
        scratch_shapes=[pltpu.VMEM((2, tt * p, LANES), F32), pltpu.VMEM((tm * p, LANES), F32),
                        pltpu.SemaphoreType.DMA((2,)), pltpu.SemaphoreType.DMA(())],
    )
    return pl.pallas_call(
        functools.partial(_dispatch_kernel, tt=tt, n_lat=n_lat, tm=tm),
        grid_spec=grid_spec,
        out_shape=jax.ShapeDtypeStruct((n_rows * p, LANES), F32),
        compiler_params=_params(("arbitrary", "arbitrary")),
        name="moe_dispatch",
    )(fill, n_used, xs, mod3, mod3, g2, slot_rows)


def _experts_kernel(te_ref, nu_ref, xs_ref, wg_ref, wu_ref, wd_ref, o_ref, h_ref, acc_ref, *, tm):
    t = pl.program_id(0)
    f = pl.program_id(1)
    p = o_ref.shape[0] // tm
    live = t < nu_ref[0]

    @pl.when(live & (f == 0))
    def _():
        h_ref[...] = jnp.concatenate([xs_ref[pl.ds(s, tm, stride=p), :] for s in range(p)],
                                     axis=-1).astype(BF16)
        acc_ref[...] = jnp.zeros_like(acc_ref)

    @pl.when(live)
    def _():
        h = h_ref[...]
        g = jnp.dot(h, wg_ref[0], preferred_element_type=F32)
        u = jnp.dot(h, wu_ref[0], preferred_element_type=F32)
        act = (g * jax.nn.sigmoid(g) * u).astype(BF16)
        acc_ref[...] += jnp.dot(act, wd_ref[0], preferred_element_type=F32)

    @pl.when(live & (f == pl.num_programs(1) - 1))
    def _():
        for s in range(p):
            o_ref[pl.ds(s, tm, stride=p), :] = acc_ref[:, s * LANES:(s + 1) * LANES]

    @pl.when(jnp.logical_not(live) & (f == pl.num_programs(1) - 1))
    def _():
        o_ref[...] = jnp.zeros_like(o_ref)


def _experts(xsort, tile_expert, n_used, wg, wu, wd, i, tm, n_tiles):
    _, n_exp, d, ffn = wg.shape
    p = d // LANES
    tf = _pick_tile(ffn, FFN_TILE)
    nf = ffn // tf

    def fcol(t, f, te, nu):
        return jnp.where(t < nu[0], f, nf - 1)

    grid_spec = pltpu.PrefetchScalarGridSpec(
        num_scalar_prefetch=2,
        grid=(n_tiles, nf),
        in_specs=[pl.BlockSpec((tm * p, LANES), lambda t, f, te, nu: (t, 0)),
                  pl.BlockSpec((None, 1, d, tf), lambda t, f, te, nu: (i, te[t], 0, fcol(t, f, te, nu))),
                  pl.BlockSpec((None, 1, d, tf), lambda t, f, te, nu: (i, te[t], 0, fcol(t, f, te, nu))),
                  pl.BlockSpec((None, 1, tf, d), lambda t, f, te, nu: (i, te[t], fcol(t, f, te, nu), 0))],
        out_specs=pl.BlockSpec((tm * p, LANES), lambda t, f, te, nu: (t, 0)),
        scratch_shapes=[pltpu.VMEM((tm, d), BF16), pltpu.VMEM((tm, d), F32)],
    )
    return pl.pallas_call(
        functools.partial(_experts_kernel, tm=tm),
        grid_spec=grid_spec,
        out_shape=jax.ShapeDtypeStruct((n_tiles * tm * p, LANES), F32),
        compiler_params=_params(("arbitrary", "arbitrary")),
        name="moe_experts",
    )(tile_expert, n_used, xsort, wg, wu, wd)


def _combine_kernel(x_ref, modl_ref, modc_ref, w_ref, info0_ref, infon_ref, y_ref, fg_ref,
                    o_ref, buf1_ref, buf2_ref, sems, *, tt, n_lat, final):
    nb = pl.num_programs(1)
    step = pl.program_id(0) * nb + pl.program_id(1)
    nsteps = pl.num_programs(0) * nb
    slot = lax.rem(step, 2)
    d = x_ref.shape[-1]
    p = d // LANES
    bufs = (buf1_ref, buf2_ref)

    def issue_block(info_ref, sl):
        def issue(r, carry):
            for k in range(2):
                pltpu.make_async_copy(y_ref.at[pl.ds(_slot_row(info_ref, k, r, p), p)],
                                      bufs[k].at[sl, pl.ds(pl.multiple_of(r * p, p), p)], sems.at[sl]).start()
            return carry
        lax.fori_loop(0, tt, issue, 0, unroll=8)

    @pl.when(step == 0)
    def _():
        issue_block(info0_ref, 0)

    @pl.when(step + 1 < nsteps)
    def _():
        issue_block(infon_ref, 1 - slot)

    for k in range(2):
        pltpu.make_async_copy(y_ref.at[pl.ds(0, tt * p)], bufs[k].at[slot], sems.at[slot]).wait()

    is_ctx = _is_ctx_rows(pl.program_id(1), tt, n_lat)
    w = w_ref[0]
    w1, w2 = w[:, 0:1], w[:, 1:2]
    outs = []
    for s in range(p):
        cols = slice(s * LANES, (s + 1) * LANES)
        y = w1 * buf1_ref[slot, pl.ds(s, tt, stride=p), :] + w2 * buf2_ref[slot, pl.ds(s, tt, stride=p), :]
        gate = jnp.where(is_ctx, modc_ref[:, 5 * d + s * LANES:5 * d + (s + 1) * LANES],
                         modl_ref[:, 5 * d + s * LANES:5 * d + (s + 1) * LANES])
        xn = x_ref[0, :, cols] + gate * y
        if final:
            outs.append(xn)
        else:
            o_ref[0, :, cols] = xn
    if final:
        ss = sum(jnp.sum(xn * xn, axis=-1, keepdims=True) for xn in outs)
        r = lax.rsqrt(ss * (1.0 / d) + NORM_EPS)
        for s, xn in enumerate(outs):
            cols = slice(s * LANES, (s + 1) * LANES)
            o_ref[0, :, cols] = xn * r * fg_ref[:, cols]


def _combine(xs, mod3, layer, wts, slot_rows, y, final_g, n_lat, nb, tt, final):
    n_batch, _, d = xs.shape
    p = d // LANES
    nblk = n_batch * nb
    out_rows = nb * tt
    grid_spec = pltpu.PrefetchScalarGridSpec(
        num_scalar_prefetch=0,
        grid=(n_batch, nb),
        in_specs=[pl.BlockSpec((1, tt, d), lambda b, j, *_: (b, j, 0)),
                  pl.BlockSpec((None, 1, 6 * d), lambda b, j, *_: (layer * MOD_ROWS + b, 0, 0)),
                  pl.BlockSpec((None, 1, 6 * d), lambda b, j, *_: (layer * MOD_ROWS + n_batch, 0, 0)),
                  pl.BlockSpec((1, tt, LANES), lambda b, j, *_: (b, j, 0)),
                  pl.BlockSpec((None, 2, tt), lambda b, j, *_: (0, 0, 0), memory_space=pltpu.SMEM),
                  pl.BlockSpec((None, 2, tt), lambda b, j, *_: (jnp.minimum(b * nb + j + 1, nblk - 1), 0, 0),
                               memory_space=pltpu.SMEM),
                  pl.BlockSpec(memory_space=pl.ANY),
                  pl.BlockSpec((1, d), lambda b, j, *_: (0, 0))],
        out_specs=pl.BlockSpec((1, tt, d), lambda b, j, *_: (b, j, 0)),
        scratch_shapes=[pltpu.VMEM((2, tt * p, LANES), F32), pltpu.VMEM((2, tt * p, LANES), F32),
                        pltpu.SemaphoreType.DMA((2,))],
    )
    return pl.pallas_call(
        functools.partial(_combine_kernel, tt=tt, n_lat=n_lat, final=final),
        grid_spec=grid_spec,
        out_shape=jax.ShapeDtypeStruct((n_batch, out_rows, d), F32),
        compiler_params=_params(("arbitrary", "arbitrary")),
        name="moe_combine",
    )(xs, mod3, mod3, wts, slot_rows, slot_rows, y, final_g)


def _moe(xs, mod3, layer, g2, router_w, wg, wu, wd, i, final_g, n_lat, final):
    n_batch, t_all, d = xs.shape
    n_exp = wg.shape[1]
    tt = TQ
    tm = 512
    rows = n_lat if final else t_all
    nb = rows // tt
    n_assign = 2 * n_batch * nb * tt
    n_tiles = -(-n_assign // tm) + n_exp
    tr = _pick_tile(rows, 1024)
    assert tr % tt == 0
    info, wts, cnt = _route(xs, mod3, layer, g2, router_w, n_lat, rows // tr, tr)
    info = info.reshape(-1, 8, tr // tt, tt).transpose(0, 2, 1, 3).reshape(-1, 8, tt)
    counts = cnt[0, :n_exp].astype(jnp.int32)
    tiles_per = (counts + tm - 1) // tm
    tile_end = jnp.cumsum(tiles_per)
    starts = (tile_end - tiles_per) * tm
    n_used = tile_end[-1:]
    tile_ids = jnp.arange(n_tiles, dtype=jnp.int32)
    tile_expert = jnp.minimum(jnp.sum(tile_ids[:, None] >= tile_end[None, :], axis=1), n_exp - 1).astype(jnp.int32)
    expert, rank = info[:, 0:2, :], info[:, 2:4, :]
    group_start = jnp.sum(jnp.where(expert[..., None] == jnp.arange(n_exp, dtype=jnp.int32), starts, 0), axis=-1)
    slot_rows = (group_start + rank) * (d // LANES)
    xsort = _dispatch(xs, mod3, layer, g2, slot_rows, starts + counts, n_used, n_lat, nb, tt, tm,
                      (n_tiles + 1) * tm)
    y = _experts(xsort, tile_expert, n_used, wg, wu, wd, i, tm, n_tiles)
    return _combine(xs, mod3, layer, wts, slot_rows, y, final_g, n_lat, nb, tt, final)


def _final_kernel(x_ref, g_ref, o_ref):
    x = x_ref[0]
    o_ref[0] = x * lax.rsqrt(jnp.mean(x * x, axis=-1, keepdims=True) + NORM_EPS) * g_ref[...]


def _final_norm(xs, g, n_lat):
    n_batch, _, d = xs.shape
    tt = _pick_tile(n_lat, 256)
    return pl.pallas_call(
        _final_kernel,
        grid=(n_batch, n_lat // tt),
        in_specs=[pl.BlockSpec((1, tt, d), lambda b, t: (b, t, 0)), pl.BlockSpec((1, d), lambda b, t: (0, 0))],
        out_specs=pl.BlockSpec((1, tt, d), lambda b, t: (b, t, 0)),
        out_shape=jax.ShapeDtypeStruct((n_batch, n_lat, d), F32),
        compiler_params=_params(("arbitrary", "arbitrary")),
        name="final_norm",
    )(xs, g)


def kernel(x, c, ctx, c_ctx, ada_w, ada_b, norm1_g, norm2_g, ev_w_in, ev_conv_w, ev_ln_g, ev_ln_b,
           ev_q_norm_g, ev_k_norm_g, ev_w_out, ev_ffn_wg, ev_ffn_wu, ev_ffn_wd, od_w_in, od_lam,
           od_subln_g, od_w_out, od_router_w, od_moe_wg, od_moe_wu, od_moe_wd, final_norm_g):
    n_batch, n_lat, d = x.shape
    n_ctx = ctx.shape[1]
    depth = ada_w.shape[0]
    assert n_batch + 1 <= MOD_ROWS and n_lat % n_ctx == 0 and n_ctx % TQ == 0 and d % LANES == 0

    cond = jnp.zeros((MOD_ROWS, d), F32).at[:n_batch].set(c).at[n_batch].set(c_ctx)
    mod3 = _adaln(cond, ada_w, ada_b).reshape(depth * MOD_ROWS, 1, 6 * d)
    tabs = _rope_tables(n_lat, n_ctx)
    hs = _head_sum_matrix()
    xs = jnp.concatenate([x, ctx], axis=1)
    ffn_w = [w.astype(BF16) for w in (ev_ffn_wg, ev_ffn_wu, ev_ffn_wd)]
    moe_w = [w.astype(BF16) for w in (od_moe_wg, od_moe_wu, od_moe_wd)]

    for layer in range(depth):
        i = layer // 2
        g1 = norm1_g[layer].reshape(1, d)
        g2 = norm2_g[layer].reshape(1, d)
        if layer % 2 == 0:
            rep = LANES // HEAD_DIM
            qg = (jnp.tile(ev_q_norm_g[i], rep) * ATT_SCALE).reshape(1, LANES)
            kg = jnp.tile(ev_k_norm_g[i], rep).reshape(1, LANES)
            ag, q, k, v = _even_in(xs, mod3, layer, g1, ev_w_in[i].astype(BF16), qg, kg, tabs, hs, n_lat)
            att = _gqa(q, k, v, n_lat, n_ctx)
            cv = _conv(ag, ev_conv_w[i], ev_ln_g[i].reshape(1, -1), ev_ln_b[i].reshape(1, -1), n_lat, n_ctx)
            xs = _ffn(xs, mod3, layer, g2, cv, att, ev_w_out[i].astype(BF16), *ffn_w, i, n_lat)
        else:
            lam_init = 0.8 - 0.6 * math.exp(-0.3 * layer)
            q, k, v = _odd_in(xs, mod3, layer, g1, od_w_in[i].astype(BF16), tabs, n_lat)
            final = layer == depth - 1
            att = _diff_attn(q, k, v, od_lam[i], od_subln_g[i].reshape(1, -1), lam_init, n_lat, n_ctx,
                             need_ctx=not final)
            xs = _out_proj(xs, mod3, layer, [att], od_w_out[i].astype(BF16), n_lat)
            xs = _moe(xs, mod3, layer, g2, od_router_w[i], *moe_w, i, final_norm_g.reshape(1, d),
                      n_lat, final=final)
    if depth % 2 == 0:
        return xs
    return _final_norm(xs, final_norm_g.reshape(1, d), n_lat)
```
